```python
import jax, jax.numpy as jnp
from jax import lax
import numpy as np

D_MODEL = 1024
BATCH = 8
SEQ = 4096
DEPTH = 2
DEC_BATCH = 16
DEC_SEQ = 2048
PAST_LEN = 128

D_FF = 2816
BLOCK = 128
SGU_WIDTH = 512
SGU_GROUPS = 4
SGU_GROUP_DIM = SGU_WIDTH // SGU_GROUPS
HEAD_DIM = 64
ATT_HEADS = 8
ATT_KV_HEADS = 2
ATT_REP = ATT_HEADS // ATT_KV_HEADS
WINDOW = 128
ROPE_THETA = 500000.0
ROPE_DIM = HEAD_DIM // 4
AB_IN = 2 * SGU_WIDTH + (ATT_HEADS + 2 * ATT_KV_HEADS) * HEAD_DIM
AB_OUT = SGU_WIDTH + ATT_HEADS * HEAD_DIM
NA_HEADS = 16
NA_WIDTH = NA_HEADS * HEAD_DIM
NA_KH = 8
NA_KW = 16
GRID_W = 64
EPS = 1e-6
NEG_INF = -1e30

kernel_name = "hybrid_bidir_encoder_gmlp_swa_natten_macaron"


def rmsnorm(x, g):
    x32 = x.astype(jnp.float32)
    y = x32 * lax.rsqrt(jnp.mean(x32 * x32, axis=-1, keepdims=True) + EPS)
    return y.astype(x.dtype) * g


def layer_norm(x, g, b):
    x32 = x.astype(jnp.float32)
    mu = jnp.mean(x32, axis=-1, keepdims=True)
    var = jnp.mean(jnp.square(x32 - mu), axis=-1, keepdims=True)
    y = (x32 - mu) * lax.rsqrt(var + EPS)
    return y.astype(x.dtype) * g + b


def swiglu(h, wg, wu, wd):
    return (jax.nn.silu(h @ wg) * (h @ wu)) @ wd


def rope_partial(x):
    L = x.shape[1]
    inv = jnp.power(ROPE_THETA, -jnp.arange(0, ROPE_DIM, 2, dtype=jnp.float32) / ROPE_DIM)
    ang = jnp.arange(L, dtype=jnp.float32)[:, None] * inv[None, :]
    cos = jnp.cos(ang)[None, :, None, :]
    sin = jnp.sin(ang)[None, :, None, :]
    xr = x[..., :ROPE_DIM].astype(jnp.float32)
    x1, x2 = xr[..., :ROPE_DIM // 2], xr[..., ROPE_DIM // 2:]
    rot = jnp.concatenate([x1 * cos - x2 * sin, x2 * cos + x1 * sin], axis=-1).astype(x.dtype)
    return jnp.concatenate([rot, x[..., ROPE_DIM:]], axis=-1)


def window_gqa(q, k, v, sink):
    B, L = q.shape[0], q.shape[1]
    nb = L // BLOCK
    qb = q.reshape(B, nb, BLOCK, ATT_KV_HEADS, ATT_REP, HEAD_DIM)
    pad = ((0, 0), (BLOCK, BLOCK), (0, 0), (0, 0))
    kp = jnp.pad(k, pad).reshape(B, nb + 2, BLOCK, ATT_KV_HEADS, HEAD_DIM)
    vp = jnp.pad(v, pad).reshape(B, nb + 2, BLOCK, ATT_KV_HEADS, HEAD_DIM)
    kw = jnp.concatenate([kp[:, :-2], kp[:, 1:-1], kp[:, 2:]], axis=2)
    vw = jnp.concatenate([vp[:, :-2], vp[:, 1:-1], vp[:, 2:]], axis=2)
    s = jnp.einsum('bnqgrd,bnkgd->bngrqk', qb, kw).astype(jnp.float32) * (HEAD_DIM ** -0.5)
    qpos = jnp.arange(nb)[:, None] * BLOCK + jnp.arange(BLOCK)[None, :]
    kpos = (jnp.arange(nb)[:, None] - 1) * BLOCK + jnp.arange(3 * BLOCK)[None, :]
    rel = kpos[:, None, :] - qpos[:, :, None]
    valid = (jnp.abs(rel) <= WINDOW) & (kpos[:, None, :] >= 0) & (kpos[:, None, :] < L)
    s = jnp.where(valid[None, :, None, None], s, NEG_INF)
    sink_b = sink.astype(jnp.float32).reshape(ATT_KV_HEADS, ATT_REP)[None, None, :, :, None, None]
    sink_b = jnp.broadcast_to(sink_b, s.shape[:-1] + (1,))
    p = jax.nn.softmax(jnp.concatenate([s, sink_b], axis=-1), axis=-1)[..., :-1]
    o = jnp.einsum('bngrqk,bnkgd->bnqgrd', p.astype(v.dtype), vw)
    return o.reshape(B, L, ATT_HEADS * HEAD_DIM)


def neighbourhood_attn(q, k, v, rpb):
    B, L = q.shape[0], q.shape[1]
    rows = L // GRID_W
    kh = min(NA_KH, rows)
    qg = q.reshape(B, rows, GRID_W, NA_HEADS, HEAD_DIM)
    kg = k.reshape(B, rows, GRID_W, NA_HEADS, HEAD_DIM)
    vg = v.reshape(B, rows, GRID_W, NA_HEADS, HEAD_DIM)
    col_start = jnp.clip(jnp.arange(GRID_W) - NA_KW // 2, 0, GRID_W - NA_KW)
    col_idx = col_start[:, None] + jnp.arange(NA_KW)[None, :]
    dc = col_idx - jnp.arange(GRID_W)[:, None] + (NA_KW - 1)
    row_start = jnp.clip(jnp.arange(rows) - kh // 2, 0, rows - kh)

    def one_row(args):
        r, rs, q_row = args
        k_band = lax.dynamic_slice_in_dim(kg, rs, kh, axis=1)
        v_band = lax.dynamic_slice_in_dim(vg, rs, kh, axis=1)
        k_win = k_band[:, :, col_idx]
        v_win = v_band[:, :, col_idx]
        dr = rs + jnp.arange(kh) - r + (NA_KH - 1)
        bias = rpb[:, dr][:, :, dc]
        s = jnp.einsum('bchd,bacjhd->bhcaj', q_row, k_win).astype(jnp.float32) * (HEAD_DIM ** -0.5)
        s = s + jnp.transpose(bias, (0, 2, 1, 3)).astype(jnp.float32)[None]
        p = jax.nn.softmax(s.reshape(B, NA_HEADS, GRID_W, kh * NA_KW), axis=-1)
        p = p.reshape(B, NA_HEADS, GRID_W, kh, NA_KW).astype(v.dtype)
        return jnp.einsum('bhcaj,bacjhd->bchd', p, v_win)

    out = lax.map(one_row, (jnp.arange(rows), row_start, jnp.transpose(qg, (1, 0, 2, 3, 4))))
    return jnp.transpose(out, (1, 0, 2, 3, 4)).reshape(B, L, NA_WIDTH)


def mixer_ab(h, w_in, sgu_ln_g, sgu_ln_b, sgu_w, sgu_b, sink, w_out):
    B, L, _ = h.shape
    z = h @ w_in
    zs = jax.nn.gelu(z[..., :2 * SGU_WIDTH], approximate=False)
    u, vh = zs[..., :SGU_WIDTH], zs[..., SGU_WIDTH:]
    vh = layer_norm(vh, sgu_ln_g, sgu_ln_b)
    nc = L // BLOCK
    vh = vh.reshape(B, nc, BLOCK, SGU_GROUPS, SGU_GROUP_DIM)
    vmix = jnp.einsum('gij,bnjgc->bnigc', sgu_w, vh) + jnp.transpose(sgu_b)[None, None, :, :, None]
    a_out = u * vmix.reshape(B, L, SGU_WIDTH)
    o = 2 * SGU_WIDTH
    q = z[..., o:o + ATT_HEADS * HEAD_DIM].reshape(B, L, ATT_HEADS, HEAD_DIM)
    o = o + ATT_HEADS * HEAD_DIM
    k = z[..., o:o + ATT_KV_HEADS * HEAD_DIM].reshape(B, L, ATT_KV_HEADS, HEAD_DIM)
    o = o + ATT_KV_HEADS * HEAD_DIM
    v = z[..., o:o + ATT_KV_HEADS * HEAD_DIM].reshape(B, L, ATT_KV_HEADS, HEAD_DIM)
    b_out = window_gqa(rope_partial(q), rope_partial(k), v, sink)
    return jnp.concatenate([a_out, b_out], axis=-1) @ w_out


def mixer_c(h, w_qkv, rpb, w_out):
    B, L, _ = h.shape
    qkv = (h @ w_qkv).reshape(B, L, 3, NA_HEADS, HEAD_DIM)
    o = neighbourhood_attn(qkv[:, :, 0], qkv[:, :, 1], qkv[:, :, 2], rpb)
    return o @ w_out


def even_layer(x, f1_n, f1_g, f1_u, f1_d, mix_n, w_in, sgu_ln_g, sgu_ln_b, sgu_w, sgu_b, sink, w_out,
               f2_n, f2_g, f2_u, f2_d):
    x = x + 0.5 * swiglu(rmsnorm(x, f1_n), f1_g, f1_u, f1_d)
    x = x + mixer_ab(rmsnorm(x, mix_n), w_in, sgu_ln_g, sgu_ln_b, sgu_w, sgu_b, sink, w_out)
    x = x + 0.5 * swiglu(rmsnorm(x, f2_n), f2_g, f2_u, f2_d)
    return x


def odd_layer(x, f1_n, f1_g, f1_u, f1_d, mix_n, w_qkv, rpb, w_out, f2_n, f2_g, f2_u, f2_d):
    x = x + 0.5 * swiglu(rmsnorm(x, f1_n), f1_g, f1_u, f1_d)
    x = x + mixer_c(rmsnorm(x, mix_n), w_qkv, rpb, w_out)
    x = x + 0.5 * swiglu(rmsnorm(x, f2_n), f2_g, f2_u, f2_d)
    return x


def trunk(x, layer_params, final_norm):
    for layer in range(DEPTH):
        if layer % 2 == 0:
            x = even_layer(x, *layer_params[layer])
        else:
            x = odd_layer(x, *layer_params[layer])
    return rmsnorm(x, final_norm)


def _w(key, shape, fan_in):
    return jax.random.normal(key, shape, jnp.float32) * (fan_in ** -0.5)


def _gain(key, shape):
    return 1.0 + 0.02 * jax.random.normal(key, shape, jnp.float32)


def setup_inputs(seed: int = 0) -> dict:
    key = jax.random.key(seed)
    ks = jax.random.split(key, 40)
    d = {}
    d['x_prompt'] = jax.random.normal(ks[0], (BATCH, SEQ, D_MODEL), jnp.float32)
    d['x_sample'] = jax.random.normal(ks[1], (DEC_BATCH, DEC_SEQ, D_MODEL), jnp.float32)
    d['l0_ffn1_norm'] = _gain(ks[2], (D_MODEL,))
    d['l0_ffn1_w_gate'] = _w(ks[3], (D_MODEL, D_FF), D_MODEL)
    d['l0_ffn1_w_up'] = _w(ks[4], (D_MODEL, D_FF), D_MODEL)
    d['l0_ffn1_w_down'] = _w(ks[5], (D_FF, D_MODEL), D_FF)
    d['l0_mix_norm'] = _gain(ks[6], (D_MODEL,))
    d['l0_w_in'] = _w(ks[7], (D_MODEL, AB_IN), D_MODEL)
    d['l0_sgu_ln_g'] = _gain(ks[8], (SGU_WIDTH,))
    d['l0_sgu_ln_b'] = 0.02 * jax.random.normal(ks[9], (SGU_WIDTH,), jnp.float32)
    d['l0_sgu_w'] = _w(ks[10], (SGU_GROUPS, BLOCK, BLOCK), BLOCK)
    d['l0_sgu_b'] = _gain(ks[11], (SGU_GROUPS, BLOCK))
    d['l0_sink'] = 0.5 * jax.random.normal(ks[12], (ATT_HEADS,), jnp.float32)
    d['l0_w_out'] = _w(ks[13], (AB_OUT, D_MODEL), AB_OUT)
    d['l0_ffn2_norm'] = _gain(ks[14], (D_MODEL,))
    d['l0_ffn2_w_gate'] = _w(ks[15], (D_MODEL, D_FF), D_MODEL)
    d['l0_ffn2_w_up'] = _w(ks[16], (D_MODEL, D_FF), D_MODEL)
    d['l0_ffn2_w_down'] = _w(ks[17], (D_FF, D_MODEL), D_FF)
    d['l1_ffn1_norm'] = _gain(ks[18], (D_MODEL,))
    d['l1_ffn1_w_gate'] = _w(ks[19], (D_MODEL, D_FF), D_MODEL)
    d['l1_ffn1_w_up'] = _w(ks[20], (D_MODEL, D_FF), D_MODEL)
    d['l1_ffn1_w_down'] = _w(ks[21], (D_FF, D_MODEL), D_FF)
    d['l1_mix_norm'] = _gain(ks[22], (D_MODEL,))
    d['l1_w_qkv'] = _w(ks[23], (D_MODEL, 3 * NA_WIDTH), D_MODEL)
    d['l1_rpb'] = 0.5 * jax.random.normal(ks[24], (NA_HEADS, 2 * NA_KH - 1, 2 * NA_KW - 1), jnp.float32)
    d['l1_w_out'] = _w(ks[25], (NA_WIDTH, D_MODEL), NA_WIDTH)
    d['l1_ffn2_norm'] = _gain(ks[26], (D_MODEL,))
    d['l1_ffn2_w_gate'] = _w(ks[27], (D_MODEL, D_FF), D_MODEL)
    d['l1_ffn2_w_up'] = _w(ks[28], (D_MODEL, D_FF), D_MODEL)
    d['l1_ffn2_w_down'] = _w(ks[29], (D_FF, D_MODEL), D_FF)
    d['final_norm'] = _gain(ks[30], (D_MODEL,))
    return d


def reference(x_prompt, x_sample,
              l0_ffn1_norm, l0_ffn1_w_gate, l0_ffn1_w_up, l0_ffn1_w_down,
              l0_mix_norm, l0_w_in, l0_sgu_ln_g, l0_sgu_ln_b, l0_sgu_w, l0_sgu_b, l0_sink, l0_w_out,
              l0_ffn2_norm, l0_ffn2_w_gate, l0_ffn2_w_up, l0_ffn2_w_down,
              l1_ffn1_norm, l1_ffn1_w_gate, l1_ffn1_w_up, l1_ffn1_w_down,
              l1_mix_norm, l1_w_qkv, l1_rpb, l1_w_out,
              l1_ffn2_norm, l1_ffn2_w_gate, l1_ffn2_w_up, l1_ffn2_w_down,
              final_norm):
    p0 = (l0_ffn1_norm, l0_ffn1_w_gate, l0_ffn1_w_up, l0_ffn1_w_down,
          l0_mix_norm, l0_w_in, l0_sgu_ln_g, l0_sgu_ln_b, l0_sgu_w, l0_sgu_b, l0_sink, l0_w_out,
          l0_ffn2_norm, l0_ffn2_w_gate, l0_ffn2_w_up, l0_ffn2_w_down)
    p1 = (l1_ffn1_norm, l1_ffn1_w_gate, l1_ffn1_w_up, l1_ffn1_w_down,
          l1_mix_norm, l1_w_qkv, l1_rpb, l1_w_out,
          l1_ffn2_norm, l1_ffn2_w_gate, l1_ffn2_w_up, l1_ffn2_w_down)
    layer_params = [p0, p1]
    y_prompt = trunk(x_prompt, layer_params, final_norm)
    y_sample = trunk(x_sample, layer_params, final_norm)
    return (y_prompt, y_sample)
```

```python
import functools

import jax
import jax.numpy as jnp
from jax import lax
from jax.experimental import pallas as pl
from jax.experimental.pallas import tpu as pltpu

D_MODEL = 1024
D_FF = 2816
BLOCK = 128
SGU_WIDTH = 512
SGU_GROUPS = 4
HEAD_DIM = 64
ATT_HEADS = 8
ATT_KV_HEADS = 2
ATT_REP = ATT_HEADS // ATT_KV_HEADS
WINDOW = 128
ROPE_THETA = 500000.0
ROPE_DIM = HEAD_DIM // 4
ROPE_HALF = ROPE_DIM // 2
Q_WIDTH = ATT_HEADS * HEAD_DIM
KV_WIDTH = ATT_KV_HEADS * HEAD_DIM
AB_IN = 2 * SGU_WIDTH + Q_WIDTH + 2 * KV_WIDTH
NA_HEADS = 16
NA_WIDTH = NA_HEADS * HEAD_DIM
NA_KH = 8
NA_KW = 16
GRID_W = 64
EPS = 1e-6
NEG_INF = -1e30
SCALE = HEAD_DIM ** -0.5
SQRT_HALF = 0.5 ** 0.5

LANES = 128
PAIR = LANES // HEAD_DIM
NA_PAIRS = NA_HEADS // PAIR
RPB_H = 2 * NA_KH - 1
RPB_W = 2 * NA_KW - 1
NA_TAB = RPB_H - 1

FFN_TM = 512
FFN_FC = 256
ATT_TQ = 512
NA_TQ = 512
VMEM_LIMIT = 56 * 1024 * 1024

F32 = jnp.float32
BF16 = jnp.bfloat16


def _params(*sem):
    return pltpu.CompilerParams(dimension_semantics=sem, vmem_limit_bytes=VMEM_LIMIT)


def _const_spec(shape):
    nd = len(shape)
    return pl.BlockSpec(shape, lambda *_: (0,) * nd, pipeline_mode=pl.Buffered(1))


def _rms(x, g):
    return x * lax.rsqrt(jnp.mean(x * x, axis=-1, keepdims=True) + EPS) * g


def _dot(a, b):
    return jnp.dot(a, b, preferred_element_type=F32)


def _dot_nt(a, b):
    return lax.dot_general(a, b, (((1,), (1,)), ((), ())), preferred_element_type=F32)


def _low_head_lanes(shape):
    return lax.broadcasted_iota(jnp.int32, shape, len(shape) - 1) % LANES < HEAD_DIM


def _ffn_kernel(x_ref, n_ref, wg_ref, wu_ref, wd_ref, fn_ref, o_ref, a_ref, *, final):
    x = x_ref[...]
    h = _rms(x, n_ref[...]).astype(BF16)
    for j in range(D_FF // FFN_FC):
        sl = slice(j * FFN_FC, (j + 1) * FFN_FC)
        g = _dot(h, wg_ref[:, sl])
        u = _dot(h, wu_ref[:, sl])
        a_ref[:, sl] = (g * jax.nn.sigmoid(g) * u).astype(BF16)
    y = x + 0.5 * _dot(a_ref[...], wd_ref[...])
    if final:
        y = _rms(y, fn_ref[...])
    o_ref[...] = y


def _ffn(x, norm, wg, wu, wd, final_norm, *, final):
    n = x.shape[0]
    tile = pl.BlockSpec((FFN_TM, D_MODEL), lambda i: (i, 0))
    return pl.pallas_call(
        functools.partial(_ffn_kernel, final=final),
        grid=(n // FFN_TM,),
        in_specs=[tile, _const_spec((1, D_MODEL)), _const_spec((D_MODEL, D_FF)),
                  _const_spec((D_MODEL, D_FF)), _const_spec((D_FF, D_MODEL)),
                  _const_spec((1, D_MODEL))],
        out_specs=tile,
        out_shape=jax.ShapeDtypeStruct((n, D_MODEL), F32),
        scratch_shapes=[pltpu.VMEM((FFN_TM, D_FF), BF16)],
        compiler_params=_params("parallel"),
        name="ffn_final" if final else "ffn",
    )(x, norm, wg, wu, wd, final_norm)


def _rope_table_kernel(cos_ref, sin_ref):
    rows = cos_ref.shape[0]
    shape = (rows, LANES)
    lane = lax.broadcasted_iota(jnp.int32, shape, 1) % HEAD_DIM
    pos = (lax.broadcasted_iota(jnp.int32, shape, 0) + pl.program_id(0) * rows).astype(F32)
    expo = -(lane % ROPE_HALF).astype(F32) / ROPE_HALF
    inv = jnp.where(lane < ROPE_DIM, jnp.power(ROPE_THETA, expo), 0.0)
    ang = pos * inv
    s = jnp.sin(ang)
    cos_ref[...] = jnp.cos(ang)
    sin_ref[...] = jnp.where(lane < ROPE_HALF, -s, s)


def _rope_table(length):
    rows = 512
    spec = pl.BlockSpec((rows, LANES), lambda i: (i, 0))
    out = jax.ShapeDtypeStruct((length, LANES), F32)
    return pl.pallas_call(
        _rope_table_kernel, grid=(length // rows,), out_specs=[spec, spec], out_shape=[out, out],
        compiler_params=_params("parallel"), name="rope_table")()


def _rope(t, cos, sin):
    lane = lax.broadcasted_iota(jnp.int32, t.shape, 1) % HEAD_DIM
    partner = jnp.where(lane < ROPE_HALF,
                        pltpu.roll(t, LANES - ROPE_HALF, 1),
                        pltpu.roll(t, ROPE_HALF, 1))
    return t * cos + partner * sin


def _mix0_proj_kernel(x_ref, n_ref, win_ref, lng_ref, lnb_ref, sw_ref, sb_ref, cos_ref, sin_ref,
                      a_ref, q_ref, k_ref, v_ref):
    tm = x_ref.shape[0]
    h = _rms(x_ref[...], n_ref[...]).astype(BF16)
    z = _dot(h, win_ref[...])
    zg = z[:, :2 * SGU_WIDTH]
    zs = 0.5 * zg * (1.0 + lax.erf(zg * SQRT_HALF))
    u = zs[:, :SGU_WIDTH]
    vh = zs[:, SGU_WIDTH:]
    mu = jnp.mean(vh, axis=-1, keepdims=True)
    d = vh - mu
    var = jnp.mean(d * d, axis=-1, keepdims=True)
    vn = (d * lax.rsqrt(var + EPS) * lng_ref[...] + lnb_ref[...]).astype(BF16)
    for c in range(tm // BLOCK):
        rows = slice(c * BLOCK, (c + 1) * BLOCK)
        for g in range(SGU_GROUPS):
            cols = slice(g * LANES, (g + 1) * LANES)
            mix = _dot(sw_ref[g], vn[rows, cols]) + sb_ref[g]
            a_ref[rows, cols] = (u[rows, cols] * mix).astype(BF16)
    cos = cos_ref[...]
    sin = sin_ref[...]
    o = 2 * SGU_WIDTH
    for p in range(Q_WIDTH // LANES):
        t = z[:, o + p * LANES:o + (p + 1) * LANES]
        q_ref[:, p * LANES:(p + 1) * LANES] = (_rope(t, cos, sin) * SCALE).astype(BF16)
    o += Q_WIDTH
    k_ref[...] = _rope(z[:, o:o + KV_WIDTH], cos, sin).astype(BF16)
    o += KV_WIDTH
    v_ref[...] = z[:, o:o + KV_WIDTH].astype(BF16)


def _mix0_proj(x, norm, w_in, ln_g, ln_b, sgu_w, sgu_b, cos, sin, seq):
    n = x.shape[0]
    tm = FFN_TM
    per_seq = seq // tm

    def tile(w):
        return pl.BlockSpec((tm, w), lambda i: (i, 0))

    def out(w):
        return jax.ShapeDtypeStruct((n, w), BF16)

    pos = pl.BlockSpec((tm, LANES), lambda i: (i % per_seq, 0))
    return pl.pallas_call(
        _mix0_proj_kernel,
        grid=(n // tm,),
        in_specs=[tile(D_MODEL), _const_spec((1, D_MODEL)), _const_spec((D_MODEL, AB_IN)),
                  _const_spec((1, SGU_WIDTH)), _const_spec((1, SGU_WIDTH)),
                  _const_spec((SGU_GROUPS, BLOCK, BLOCK)), _const_spec((SGU_GROUPS, BLOCK, LANES)),
                  pos, pos],
        out_specs=[tile(SGU_WIDTH), tile(Q_WIDTH), tile(KV_WIDTH), tile(KV_WIDTH)],
        out_shape=[out(SGU_WIDTH), out(Q_WIDTH), out(KV_WIDTH), out(KV_WIDTH)],
        compiler_params=_params("parallel"),
        name="mix0_proj",
    )(x, norm, w_in, ln_g, ln_b, sgu_w, sgu_b, cos, sin)


def _mix0_attn_kernel(x_ref, a_ref, q_ref, kp_ref, kc_ref, kn_ref, vp_ref, vc_ref, vn_ref,
                      sink_ref, wo_ref, o_ref, kbuf, vbuf, bbuf, *, seq):
    tq = x_ref.shape[0]
    pos0 = pl.program_id(1) * tq
    kbuf[0:BLOCK] = kp_ref[...]
    kbuf[BLOCK:BLOCK + tq] = kc_ref[...]
    kbuf[BLOCK + tq:] = kn_ref[...]
    vbuf[0:BLOCK] = vp_ref[...]
    vbuf[BLOCK:BLOCK + tq] = vc_ref[...]
    vbuf[BLOCK + tq:] = vn_ref[...]
    nslab = ATT_REP * ATT_KV_HEADS
    kw_len = 3 * BLOCK
    low = _low_head_lanes((BLOCK, LANES))
    qi = lax.broadcasted_iota(jnp.int32, (BLOCK, kw_len), 0)
    kj = lax.broadcasted_iota(jnp.int32, (BLOCK, kw_len), 1)
    in_band = jnp.abs(kj - BLOCK - qi) <= WINDOW
    sink = sink_ref[...]
    for b in range(tq // BLOCK):
        rows = slice(b * BLOCK, (b + 1) * BLOCK)
        kpos = pos0 + (b - 1) * BLOCK + kj
        valid = in_band & (kpos >= 0) & (kpos < seq)
        slabs = []
        for r in range(ATT_REP):
            qp = q_ref[rows, r * LANES:(r + 1) * LANES]
            slabs.append(jnp.where(low, qp, jnp.zeros_like(qp)))
            slabs.append(jnp.where(low, jnp.zeros_like(qp), qp))
        s = _dot_nt(jnp.concatenate(slabs, axis=0), kbuf[b * BLOCK:b * BLOCK + kw_len])
        s = jnp.where(valid[None], s.reshape(nslab, BLOCK, kw_len), NEG_INF)
        s = s.reshape(nslab * BLOCK, kw_len)
        m = jnp.maximum(jnp.max(s, axis=-1, keepdims=True), sink)
        e = jnp.exp(s - m)
        den = jnp.sum(e, axis=-1, keepdims=True) + jnp.exp(sink - m)
        o = _dot(e.astype(BF16), vbuf[b * BLOCK:b * BLOCK + kw_len]) / den
        for r in range(ATT_REP):
            o_lo = o[(2 * r) * BLOCK:(2 * r + 1) * BLOCK]
            o_hi = o[(2 * r + 1) * BLOCK:(2 * r + 2) * BLOCK]
            bbuf[rows, r * LANES:(r + 1) * LANES] = jnp.where(low, o_lo, o_hi).astype(BF16)
    y = _dot(a_ref[...], wo_ref[:SGU_WIDTH]) + _dot(bbuf[...], wo_ref[SGU_WIDTH:])
    o_ref[...] = x_ref[...] + y


def _mix0_attn(x, a, q, k, v, sink_col, w_out):
    bsz, seq, _ = x.shape
    tq = ATT_TQ
    per = tq // BLOCK
    nblk = seq // BLOCK

    def tile(w):
        return pl.BlockSpec((None, tq, w), lambda b, i: (b, i, 0))

    prev = pl.BlockSpec((None, BLOCK, KV_WIDTH), lambda b, i: (b, jnp.maximum(i * per - 1, 0), 0))
    nxt = pl.BlockSpec((None, BLOCK, KV_WIDTH),
                       lambda b, i: (b, jnp.minimum((i + 1) * per, nblk - 1), 0))
    return pl.pallas_call(
        functools.partial(_mix0_attn_kernel, seq=seq),
        grid=(bsz, seq // tq),
        in_specs=[tile(D_MODEL), tile(SGU_WIDTH), tile(Q_WIDTH),
                  prev, tile(KV_WIDTH), nxt, prev, tile(KV_WIDTH), nxt,
                  _const_spec((ATT_HEADS * BLOCK, 1)), _const_spec((D_MODEL, D_MODEL))],
        out_specs=tile(D_MODEL),
        out_shape=jax.ShapeDtypeStruct(x.shape, F32),
        scratch_shapes=[pltpu.VMEM((tq + 2 * BLOCK, KV_WIDTH), BF16),
                        pltpu.VMEM((tq + 2 * BLOCK, KV_WIDTH), BF16),
                        pltpu.VMEM((tq, Q_WIDTH), BF16)],
        compiler_params=_params("parallel", "parallel"),
        name="mix0_attn",
    )(x, a, q, k, k, k, v, v, v, sink_col, w_out)


def _mix1_proj_kernel(x_ref, n_ref, w_ref, q_ref, k_ref, v_ref):
    h = _rms(x_ref[...], n_ref[...]).astype(BF16)
    q_ref[...] = (_dot(h, w_ref[:, :NA_WIDTH]) * SCALE).astype(BF16)
    k_ref[...] = _dot(h, w_ref[:, NA_WIDTH:2 * NA_WIDTH]).astype(BF16)
    v_ref[...] = _dot(h, w_ref[:, 2 * NA_WIDTH:]).astype(BF16)


def _mix1_proj(x, norm, w_qkv):
    n = x.shape[0]
    tile = pl.BlockSpec((FFN_TM, D_MODEL), lambda i: (i, 0))
    out = jax.ShapeDtypeStruct((n, NA_WIDTH), BF16)
    return pl.pallas_call(
        _mix1_proj_kernel,
        grid=(n // FFN_TM,),
        in_specs=[tile, _const_spec((1, D_MODEL)), _const_spec((D_MODEL, 3 * NA_WIDTH))],
        out_specs=[tile, tile, tile],
        out_shape=[out, out, out],
        compiler_params=_params("parallel"),
        name="mix1_proj",
    )(x, norm, w_qkv)


def _na_bias_kernel(rpb_ref, tab_ref):
    p = pl.program_id(0)
    shape = (2 * GRID_W, LANES)
    row = lax.broadcasted_iota(jnp.int32, shape, 0)
    lane = lax.broadcasted_iota(jnp.int32, shape, 1)
    c = row % GRID_W
    kc = lane % GRID_W
    start = jnp.clip(c - NA_KW // 2, 0, GRID_W - NA_KW)
    in_win = (kc >= start) & (kc < start + NA_KW)
    code = ((row // GRID_W) * 2 + lane // GRID_W) * 32 + (kc - c + NA_KW - 1)
    code = jnp.where(in_win, code, -1)
    for j in range(NA_TAB):
        val = jnp.full(shape, NEG_INF, F32)
        for e in range(PAIR):
            for s in range(2):
                base = ((p * PAIR + e) * RPB_H + (j + s)) * RPB_W
                for dcol in range(RPB_W):
                    val = jnp.where(code == (e * 2 + s) * 32 + dcol, rpb_ref[base + dcol], val)
        tab_ref[j] = val


def _na_bias_table(rpb):
    return pl.pallas_call(
        _na_bias_kernel,
        grid=(NA_PAIRS,),
        in_specs=[pl.BlockSpec(memory_space=pltpu.SMEM)],
        out_specs=pl.BlockSpec((None, NA_TAB, 2 * GRID_W, LANES), lambda p: (p, 0, 0, 0)),
        out_shape=jax.ShapeDtypeStruct((NA_PAIRS, NA_TAB, 2 * GRID_W, LANES), F32),
        compiler_params=_params("parallel"),
        name="na_bias_table",
    )(rpb.reshape(-1))


def _mix1_attn_kernel(x_ref, q_ref, k_ref, v_ref, tab_ref, wo_ref, o_ref, obuf, *, grid_rows):
    tq = x_ref.shape[0]
    rows_per_tile = tq // GRID_W
    band = NA_KH * GRID_W
    row0 = pl.program_id(1) * rows_per_tile
    low_q = _low_head_lanes((GRID_W, LANES))

    def one_row(rr, carry):
        r = row0 + rr
        rs = jnp.clip(r - NA_KH // 2, 0, grid_rows - NA_KH)
        delta = rs - r + (NA_KH - 1)
        k0 = pl.multiple_of(rs * GRID_W, GRID_W)
        q0 = pl.multiple_of(rr * GRID_W, GRID_W)
        for p in range(NA_PAIRS):
            cols = slice(p * LANES, (p + 1) * LANES)
            qp = q_ref[pl.ds(q0, GRID_W), cols]
            lhs = jnp.concatenate([jnp.where(low_q, qp, jnp.zeros_like(qp)),
                                   jnp.where(low_q, jnp.zeros_like(qp), qp)], axis=0)
            s = _dot_nt(lhs, k_ref[pl.ds(k0, band), cols])
            bias = jnp.concatenate([tab_ref[p, delta + 2 * t] for t in range(NA_KH // 2)], axis=1)
            s = s + bias
            m = jnp.max(s, axis=-1, keepdims=True)
            e = jnp.exp(s - m)
            den = jnp.sum(e, axis=-1, keepdims=True)
            o = _dot(e.astype(BF16), v_ref[pl.ds(k0, band), cols]) / den
            obuf[pl.ds(q0, GRID_W), cols] = jnp.where(low_q, o[:GRID_W], o[GRID_W:]).astype(BF16)
        return carry

    lax.fori_loop(0, rows_per_tile, one_row, 0)
    o_ref[...] = x_ref[...] + _dot(obuf[...], wo_ref[...])


def _mix1_attn(x, q, k, v, tab, w_out):
    bsz, seq, _ = x.shape
    tq = NA_TQ
    grid_rows = seq // GRID_W
    assert grid_rows >= NA_KH and seq % tq == 0
    tile = pl.BlockSpec((None, tq, D_MODEL), lambda b, i: (b, i, 0))
    whole = pl.BlockSpec((None, seq, NA_WIDTH), lambda b, i: (b, 0, 0),
                         pipeline_mode=pl.Buffered(1))
    return pl.pallas_call(
        functools.partial(_mix1_attn_kernel, grid_rows=grid_rows),
        grid=(bsz, seq // tq),
        in_specs=[tile, tile, whole, whole,
                  _const_spec((NA_PAIRS, NA_TAB, 2 * GRID_W, LANES)),
                  _const_spec((D_MODEL, D_MODEL))],
        out_specs=tile,
        out_shape=jax.ShapeDtypeStruct(x.shape, F32),
        scratch_shapes=[pltpu.VMEM((tq, NA_WIDTH), BF16)],
        compiler_params=_params("parallel", "arbitrary"),
        name="mix1_attn",
    )(x, q, k, v, tab, w_out)


def _prepare(l0_ffn1_norm, l0_ffn1_w_gate, l0_ffn1_w_up, l0_ffn1_w_down, l0_mix_norm, l0_w_in,
             l0_sgu_ln_g, l0_sgu_ln_b, l0_sgu_w, l0_sgu_b, l0_sink, l0_w_out, l0_ffn2_norm,
             l0_ffn2_w_gate, l0_ffn2_w_up, l0_ffn2_w_down, l1_ffn1_norm, l1_ffn1_w_gate,
             l1_ffn1_w_up, l1_ffn1_w_down, l1_mix_norm, l1_w_qkv, l1_rpb, l1_w_out, l1_ffn2_norm,
             l1_ffn2_w_gate, l1_ffn2_w_up, l1_ffn2_w_down, final_norm):
    def row(v):
        return v.reshape(1, -1)

    def ffn(n, g, u, d):
        return (row(n), g.astype(BF16), u.astype(BF16), d.astype(BF16))

    head_order = [ATT_REP * g + r for r in range(ATT_REP) for g in range(ATT_KV_HEADS)]
    q0 = 2 * SGU_WIDTH
    q_cols = jnp.concatenate([jnp.arange(HEAD_DIM) + q0 + h * HEAD_DIM for h in head_order])
    in_cols = jnp.concatenate([jnp.arange(q0), q_cols, jnp.arange(q0 + Q_WIDTH, AB_IN)])
    out_rows = jnp.concatenate([jnp.arange(SGU_WIDTH), q_cols - q0 + SGU_WIDTH])
    sink_col = jnp.repeat(l0_sink[jnp.array(head_order)], BLOCK).reshape(-1, 1)
    sgu_b = jnp.broadcast_to(l0_sgu_b[:, :, None], (SGU_GROUPS, BLOCK, LANES))
    return dict(
        l0_f1=ffn(l0_ffn1_norm, l0_ffn1_w_gate, l0_ffn1_w_up, l0_ffn1_w_down),
        l0_f2=ffn(l0_ffn2_norm, l0_ffn2_w_gate, l0_ffn2_w_up, l0_ffn2_w_down),
        l1_f1=ffn(l1_ffn1_norm, l1_ffn1_w_gate, l1_ffn1_w_up, l1_ffn1_w_down),
        l1_f2=ffn(l1_ffn2_norm, l1_ffn2_w_gate, l1_ffn2_w_up, l1_ffn2_w_down),
        l0_mix=(row(l0_mix_norm), l0_w_in[:, in_cols].astype(BF16), row(l0_sgu_ln_g),
                row(l0_sgu_ln_b), l0_sgu_w.astype(BF16), sgu_b),
        l0_sink=sink_col,
        l0_w_out=l0_w_out[out_rows].astype(BF16),
        l1_mix=(row(l1_mix_norm), l1_w_qkv.astype(BF16)),
        l1_rpb=l1_rpb,
        l1_w_out=l1_w_out.astype(BF16),
        final_norm=row(final_norm),
    )


def _trunk(x, w, cos, sin, tab):
    bsz, seq, _ = x.shape
    n = bsz * seq
    assert seq % ATT_TQ == 0 and seq % FFN_TM == 0
    fn = w["final_norm"]
    x = _ffn(x.reshape(n, D_MODEL), *w["l0_f1"], fn, final=False)
    a, q, k, v = _mix0_proj(x, *w["l0_mix"], cos, sin, seq)

    def seqs(t):
        return t.reshape(bsz, seq, t.shape[-1])

    x = _mix0_attn(seqs(x), seqs(a), seqs(q), seqs(k), seqs(v), w["l0_sink"], w["l0_w_out"])
    x = _ffn(x.reshape(n, D_MODEL), *w["l0_f2"], fn, final=False)
    x = _ffn(x, *w["l1_f1"], fn, final=False)
    q, k, v = _mix1_proj(x, *w["l1_mix"])
    x = _mix1_attn(seqs(x), seqs(q), seqs(k), seqs(v), tab, w["l1_w_out"])
    x = _ffn(x.reshape(n, D_MODEL), *w["l1_f2"], fn, final=True)
    return x.reshape(bsz, seq, D_MODEL)


def kernel(x_prompt, x_sample, l0_ffn1_norm, l0_ffn1_w_gate, l0_ffn1_w_up, l0_ffn1_w_down, l0_mix_norm, l0_w_in, l0_sgu_ln_g, l0_sgu_ln_b, l0_sgu_w, l0_sgu_b, l0_sink, l0_w_out, l0_ffn2_norm, l0_ffn2_w_gate, l0_ffn2_w_up, l0_ffn2_w_down, l1_ffn1_norm, l1_ffn1_w_gate, l1_ffn1_w_up, l1_ffn1_w_down, l1_mix_norm, l1_w_qkv, l1_rpb, l1_w_out, l1_ffn2_norm, l1_ffn2_w_gate, l1_ffn2_w_up, l1_ffn2_w_down, final_norm):
    w = _prepare(l0_ffn1_norm, l0_ffn1_w_gate, l0_ffn1_w_up, l0_ffn1_w_down, l0_mix_norm, l0_w_in,
                 l0_sgu_ln_g, l0_sgu_ln_b, l0_sgu_w, l0_sgu_b, l0_sink, l0_w_out, l0_ffn2_norm,
                 l0_ffn2_w_gate, l0_ffn2_w_up, l0_ffn2_w_down, l1_ffn1_norm, l1_ffn1_w_gate,
                 l1_ffn1_w_up, l1_ffn1_w_down, l1_mix_norm, l1_w_qkv, l1_rpb, l1_w_out,
                 l1_ffn2_norm, l1_ffn2_w_gate, l1_ffn2_w_up, l1_ffn2_w_down, final_norm)
    cos, sin = _rope_table(max(x_prompt.shape[1], x_sample.shape[1]))
    tab = _na_bias_table(l1_rpb)
    return (_trunk(x_prompt, w, cos, sin, tab), _trunk(x_sample, w, cos, sin, tab))
```

```python
import functools

import jax
import jax.numpy as jnp
from jax import lax
from jax.experimental import pallas as pl
from jax.experimental.pallas import tpu as pltpu

D_MODEL = 1024
D_FF = 2816
BLOCK = 128
SGU_WIDTH = 512
SGU_GROUPS = 4
HEAD_DIM = 64
ATT_HEADS = 8
ATT_KV_HEADS = 2
ATT_REP = ATT_HEADS // ATT_KV_HEADS
WINDOW = 128
ROPE_THETA = 500000.0
ROPE_DIM = HEAD_DIM // 4
ROPE_HALF = ROPE_DIM // 2
Q_WIDTH = ATT_HEADS * HEAD_DIM
KV_WIDTH = ATT_KV_HEADS * HEAD_DIM
AB_IN = 2 * SGU_WIDTH + Q_WIDTH + 2 * KV_WIDTH
NA_HEADS = 16
NA_WIDTH = NA_HEADS * HEAD_DIM
NA_KH = 8
NA_KW = 16
GRID_W = 64
EPS = 1e-6
NEG_INF = -1e30
SCALE = HEAD_DIM ** -0.5
SQRT_HALF = 0.5 ** 0.5
LOG2E = 1.4426950408889634

LANES = 128
MXU_DIM = 256
QUAD = MXU_DIM // HEAD_DIM
NA_QUADS = NA_HEADS // QUAD
RPB_H = 2 * NA_KH - 1
RPB_W = 2 * NA_KW - 1
RPB_CODE = 32

FFN_TM = 512
FFN_FC = 256
ATT_TQ = 512
NA_TQ = 512
VMEM_LIMIT = 56 * 1024 * 1024

F32 = jnp.float32
BF16 = jnp.bfloat16


def _params(*sem):
    return pltpu.CompilerParams(dimension_semantics=sem, vmem_limit_bytes=VMEM_LIMIT)


def _const_spec(shape):
    nd = len(shape)
    return pl.BlockSpec(shape, lambda *_: (0,) * nd, pipeline_mode=pl.Buffered(1))


def _rms(x, g):
    return x * lax.rsqrt(jnp.mean(x * x, axis=-1, keepdims=True) + EPS) * g


def _dot(a, b):
    return jnp.dot(a, b, preferred_element_type=F32)


def _dot_nt(a, b):
    return lax.dot_general(a, b, (((1,), (1,)), ((), ())), preferred_element_type=F32)


def _low_head_lanes(shape):
    return lax.broadcasted_iota(jnp.int32, shape, len(shape) - 1) % LANES < HEAD_DIM


def _ffn_kernel(x_ref, n_ref, wg_ref, wu_ref, wd_ref, fn_ref, o_ref, a_ref, *, final):
    x = x_ref[...]
    h = _rms(x, n_ref[...]).astype(BF16)
    for j in range(D_FF // FFN_FC):
        sl = slice(j * FFN_FC, (j + 1) * FFN_FC)
        g = _dot(h, wg_ref[:, sl])
        u = _dot(h, wu_ref[:, sl])
        a_ref[:, sl] = (g * jax.nn.sigmoid(g) * u).astype(BF16)
    y = x + 0.5 * _dot(a_ref[...], wd_ref[...])
    if final:
        y = _rms(y, fn_ref[...])
    o_ref[...] = y


def _ffn(x, norm, wg, wu, wd, final_norm, *, final):
    n = x.shape[0]
    tile = pl.BlockSpec((FFN_TM, D_MODEL), lambda i: (i, 0))
    return pl.pallas_call(
        functools.partial(_ffn_kernel, final=final),
        grid=(n // FFN_TM,),
        in_specs=[tile, _const_spec((1, D_MODEL)), _const_spec((D_MODEL, D_FF)),
                  _const_spec((D_MODEL, D_FF)), _const_spec((D_FF, D_MODEL)),
                  _const_spec((1, D_MODEL))],
        out_specs=tile,
        out_shape=jax.ShapeDtypeStruct((n, D_MODEL), F32),
        scratch_shapes=[pltpu.VMEM((FFN_TM, D_FF), BF16)],
        compiler_params=_params("parallel"),
        name="ffn_final" if final else "ffn",
    )(x, norm, wg, wu, wd, final_norm)


def _rope_table_kernel(cos_ref, sin_ref):
    rows = cos_ref.shape[0]
    shape = (rows, LANES)
    lane = lax.broadcasted_iota(jnp.int32, shape, 1) % HEAD_DIM
    pos = (lax.broadcasted_iota(jnp.int32, shape, 0) + pl.program_id(0) * rows).astype(F32)
    expo = -(lane % ROPE_HALF).astype(F32) / ROPE_HALF
    inv = jnp.where(lane < ROPE_DIM, jnp.power(ROPE_THETA, expo), 0.0)
    ang = pos * inv
    s = jnp.sin(ang)
    cos_ref[...] = jnp.cos(ang)
    sin_ref[...] = jnp.where(lane < ROPE_HALF, -s, s)


def _rope_table(length):
    rows = 512
    spec = pl.BlockSpec((rows, LANES), lambda i: (i, 0))
    out = jax.ShapeDtypeStruct((length, LANES), F32)
    return pl.pallas_call(
        _rope_table_kernel, grid=(length // rows,), out_specs=[spec, spec], out_shape=[out, out],
        compiler_params=_params("parallel"), name="rope_table")()


def _rope(t, cos, sin):
    lane = lax.broadcasted_iota(jnp.int32, t.shape, 1) % HEAD_DIM
    partner = jnp.where(lane < ROPE_HALF,
                        pltpu.roll(t, LANES - ROPE_HALF, 1),
                        pltpu.roll(t, ROPE_HALF, 1))
    return t * cos + partner * sin


def _mix0_proj_kernel(x_ref, n_ref, win_ref, lng_ref, lnb_ref, sw_ref, sb_ref, cos_ref, sin_ref,
                      a_ref, q_ref, k_ref, v_ref):
    tm = x_ref.shape[0]
    h = _rms(x_ref[...], n_ref[...]).astype(BF16)
    z = _dot(h, win_ref[...])
    zg = z[:, :2 * SGU_WIDTH]
    zs = 0.5 * zg * (1.0 + lax.erf(zg * SQRT_HALF))
    u = zs[:, :SGU_WIDTH]
    vh = zs[:, SGU_WIDTH:]
    mu = jnp.mean(vh, axis=-1, keepdims=True)
    d = vh - mu
    var = jnp.mean(d * d, axis=-1, keepdims=True)
    vn = (d * lax.rsqrt(var + EPS) * lng_ref[...] + lnb_ref[...]).astype(BF16)
    for c in range(tm // BLOCK):
        rows = slice(c * BLOCK, (c + 1) * BLOCK)
        for g in range(SGU_GROUPS):
            cols = slice(g * LANES, (g + 1) * LANES)
            mix = _dot(sw_ref[g], vn[rows, cols]) + sb_ref[g]
            a_ref[rows, cols] = (u[rows, cols] * mix).astype(BF16)
    cos = cos_ref[...]
    sin = sin_ref[...]
    o = 2 * SGU_WIDTH
    for p in range(Q_WIDTH // LANES):
        t = z[:, o + p * LANES:o + (p + 1) * LANES]
        q_ref[:, p * LANES:(p + 1) * LANES] = (_rope(t, cos, sin) * SCALE).astype(BF16)
    o += Q_WIDTH
    k_ref[...] = _rope(z[:, o:o + KV_WIDTH], cos, sin).astype(BF16)
    o += KV_WIDTH
    v_ref[...] = z[:, o:o + KV_WIDTH].astype(BF16)


def _mix0_proj(x, norm, w_in, ln_g, ln_b, sgu_w, sgu_b, cos, sin, seq):
    n = x.shape[0]
    tm = FFN_TM
    per_seq = seq // tm

    def tile(w):
        return pl.BlockSpec((tm, w), lambda i: (i, 0))

    def out(w):
        return jax.ShapeDtypeStruct((n, w), BF16)

    pos = pl.BlockSpec((tm, LANES), lambda i: (i % per_seq, 0))
    return pl.pallas_call(
        _mix0_proj_kernel,
        grid=(n // tm,),
        in_specs=[tile(D_MODEL), _const_spec((1, D_MODEL)), _const_spec((D_MODEL, AB_IN)),
                  _const_spec((1, SGU_WIDTH)), _const_spec((1, SGU_WIDTH)),
                  _const_spec((SGU_GROUPS, BLOCK, BLOCK)), _const_spec((SGU_GROUPS, BLOCK, LANES)),
                  pos, pos],
        out_specs=[tile(SGU_WIDTH), tile(Q_WIDTH), tile(KV_WIDTH), tile(KV_WIDTH)],
        out_shape=[out(SGU_WIDTH), out(Q_WIDTH), out(KV_WIDTH), out(KV_WIDTH)],
        compiler_params=_params("parallel"),
        name="mix0_proj",
    )(x, norm, w_in, ln_g, ln_b, sgu_w, sgu_b, cos, sin)


def _mix0_attn_kernel(x_ref, a_ref, q_ref, kp_ref, kc_ref, kn_ref, vp_ref, vc_ref, vn_ref,
                      sink_ref, wo_ref, o_ref, kbuf, vbuf, bbuf, *, seq):
    tq = x_ref.shape[0]
    pos0 = pl.program_id(1) * tq
    kbuf[0:BLOCK] = kp_ref[...]
    kbuf[BLOCK:BLOCK + tq] = kc_ref[...]
    kbuf[BLOCK + tq:] = kn_ref[...]
    vbuf[0:BLOCK] = vp_ref[...]
    vbuf[BLOCK:BLOCK + tq] = vc_ref[...]
    vbuf[BLOCK + tq:] = vn_ref[...]
    nslab = ATT_REP * ATT_KV_HEADS
    kw_len = 3 * BLOCK
    low = _low_head_lanes((BLOCK, LANES))
    qi = lax.broadcasted_iota(jnp.int32, (BLOCK, kw_len), 0)
    kj = lax.broadcasted_iota(jnp.int32, (BLOCK, kw_len), 1)
    in_band = jnp.abs(kj - BLOCK - qi) <= WINDOW
    sink = sink_ref[...]
    for b in range(tq // BLOCK):
        rows = slice(b * BLOCK, (b + 1) * BLOCK)
        kpos = pos0 + (b - 1) * BLOCK + kj
        valid = in_band & (kpos >= 0) & (kpos < seq)
        slabs = []
        for r in range(ATT_REP):
            qp = q_ref[rows, r * LANES:(r + 1) * LANES]
            slabs.append(jnp.where(low, qp, jnp.zeros_like(qp)))
            slabs.append(jnp.where(low, jnp.zeros_like(qp), qp))
        s = _dot_nt(jnp.concatenate(slabs, axis=0), kbuf[b * BLOCK:b * BLOCK + kw_len])
        s = jnp.where(valid[None], s.reshape(nslab, BLOCK, kw_len), NEG_INF)
        s = s.reshape(nslab * BLOCK, kw_len)
        m = jnp.maximum(jnp.max(s, axis=-1, keepdims=True), sink)
        e = jnp.exp(s - m)
        den = jnp.sum(e, axis=-1, keepdims=True) + jnp.exp(sink - m)
        o = _dot(e.astype(BF16), vbuf[b * BLOCK:b * BLOCK + kw_len]) / den
        for r in range(ATT_REP):
            o_lo = o[(2 * r) * BLOCK:(2 * r + 1) * BLOCK]
            o_hi = o[(2 * r + 1) * BLOCK:(2 * r + 2) * BLOCK]
            bbuf[rows, r * LANES:(r + 1) * LANES] = jnp.where(low, o_lo, o_hi).astype(BF16)
    y = _dot(a_ref[...], wo_ref[:SGU_WIDTH]) + _dot(bbuf[...], wo_ref[SGU_WIDTH:])
    o_ref[...] = x_ref[...] + y


def _mix0_attn(x, a, q, k, v, sink_col, w_out):
    bsz, seq, _ = x.shape
    tq = ATT_TQ
    per = tq // BLOCK
    nblk = seq // BLOCK

    def tile(w):
        return pl.BlockSpec((None, tq, w), lambda b, i: (b, i, 0))

    prev = pl.BlockSpec((None, BLOCK, KV_WIDTH), lambda b, i: (b, jnp.maximum(i * per - 1, 0), 0))
    nxt = pl.BlockSpec((None, BLOCK, KV_WIDTH),
                       lambda b, i: (b, jnp.minimum((i + 1) * per, nblk - 1), 0))
    return pl.pallas_call(
        functools.partial(_mix0_attn_kernel, seq=seq),
        grid=(bsz, seq // tq),
        in_specs=[tile(D_MODEL), tile(SGU_WIDTH), tile(Q_WIDTH),
                  prev, tile(KV_WIDTH), nxt, prev, tile(KV_WIDTH), nxt,
                  _const_spec((ATT_HEADS * BLOCK, 1)), _const_spec((D_MODEL, D_MODEL))],
        out_specs=tile(D_MODEL),
        out_shape=jax.ShapeDtypeStruct(x.shape, F32),
        scratch_shapes=[pltpu.VMEM((tq + 2 * BLOCK, KV_WIDTH), BF16),
                        pltpu.VMEM((tq + 2 * BLOCK, KV_WIDTH), BF16),
                        pltpu.VMEM((tq, Q_WIDTH), BF16)],
        compiler_params=_params("parallel", "parallel"),
        name="mix0_attn",
    )(x, a, q, k, k, k, v, v, v, sink_col, w_out)


def _mix1_proj_kernel(x_ref, n_ref, w_ref, q_ref, k_ref, v_ref):
    h = _rms(x_ref[...], n_ref[...]).astype(BF16)
    q_ref[...] = (_dot(h, w_ref[:, :NA_WIDTH]) * (SCALE * LOG2E)).astype(BF16)
    k_ref[...] = _dot(h, w_ref[:, NA_WIDTH:2 * NA_WIDTH]).astype(BF16)
    v_ref[...] = _dot(h, w_ref[:, 2 * NA_WIDTH:]).astype(BF16)


def _mix1_proj(x, norm, w_qkv):
    n = x.shape[0]
    tile = pl.BlockSpec((FFN_TM, D_MODEL), lambda i: (i, 0))
    out = jax.ShapeDtypeStruct((n, NA_WIDTH), BF16)
    return pl.pallas_call(
        _mix1_proj_kernel,
        grid=(n // FFN_TM,),
        in_specs=[tile, _const_spec((1, D_MODEL)), _const_spec((D_MODEL, 3 * NA_WIDTH))],
        out_specs=[tile, tile, tile],
        out_shape=[out, out, out],
        compiler_params=_params("parallel"),
        name="mix1_proj",
    )(x, norm, w_qkv)


def _na_bias_kernel(rpb_ref, tab_ref):
    quad = pl.program_id(0)
    shape = (GRID_W, MXU_DIM)
    kc = lax.broadcasted_iota(jnp.int32, shape, 0)
    lane = lax.broadcasted_iota(jnp.int32, shape, 1)
    c = lane % GRID_W
    start = jnp.clip(c - NA_KW // 2, 0, GRID_W - NA_KW)
    in_win = (kc >= start) & (kc < start + NA_KW)
    code = jnp.where(in_win, (lane // GRID_W) * RPB_CODE + (kc - c + NA_KW - 1), -1)
    for dr in range(RPB_H):
        val = jnp.full(shape, NEG_INF, F32)
        for h in range(QUAD):
            base = ((quad * QUAD + h) * RPB_H + dr) * RPB_W
            for dcol in range(RPB_W):
                val = jnp.where(code == h * RPB_CODE + dcol, rpb_ref[base + dcol] * LOG2E, val)
        tab_ref[dr * GRID_W:(dr + 1) * GRID_W, :] = val


def _na_bias_table(rpb):
    return pl.pallas_call(
        _na_bias_kernel,
        grid=(NA_QUADS,),
        in_specs=[pl.BlockSpec(memory_space=pltpu.SMEM)],
        out_specs=pl.BlockSpec((None, RPB_H * GRID_W, MXU_DIM), lambda p: (p, 0, 0)),
        out_shape=jax.ShapeDtypeStruct((NA_QUADS, RPB_H * GRID_W, MXU_DIM), F32),
        compiler_params=_params("parallel"),
        name="na_bias_table",
    )(rpb.reshape(-1))


def _mix1_attn_kernel(x_ref, q_ref, k_ref, v_ref, tab_ref, wo_ref, o_ref, obuf, sbuf, *, grid_rows):
    tq = x_ref.shape[0]
    rows_per_tile = tq // GRID_W
    band = NA_KH * GRID_W
    row0 = pl.program_id(1) * rows_per_tile
    lane_head = lax.broadcasted_iota(jnp.int32, (GRID_W, MXU_DIM), 1) // HEAD_DIM
    low = _low_head_lanes((GRID_W, LANES))
    ones = jnp.ones((band, LANES), BF16)

    half = band // 2

    def geometry(rr):
        r = row0 + rr
        rs = jnp.clip(r - NA_KH // 2, 0, grid_rows - NA_KH)
        return (pl.multiple_of(rs * GRID_W, GRID_W), pl.multiple_of(rr * GRID_W, GRID_W),
                pl.multiple_of((rs - r + NA_KH - 1) * GRID_W, GRID_W))

    def scores(geom, quad):
        k0, q0, t0 = geom
        cols = slice(quad * MXU_DIM, (quad + 1) * MXU_DIM)
        qq = q_ref[pl.ds(q0, GRID_W), cols]
        qs = jnp.concatenate([jnp.where(lane_head == h, qq, jnp.zeros_like(qq))
                              for h in range(QUAD)], axis=0)
        return [_dot_nt(k_ref[pl.ds(k0 + i * half, half), cols], qs)
                + tab_ref[quad, pl.ds(t0 + i * half, half), :] for i in range(2)]

    def probs(st):
        st = jnp.concatenate(st, axis=0)
        return jnp.exp2(st - jnp.max(st, axis=0, keepdims=True)).astype(BF16)

    def values(geom, quad, e):
        k0, q0, _ = geom
        for t in range(QUAD // 2):
            pair = slice(quad * MXU_DIM + t * LANES, quad * MXU_DIM + (t + 1) * LANES)
            rhs = jnp.concatenate([v_ref[pl.ds(k0, band), pair], ones], axis=1)
            o = lax.dot_general(e[:, t * LANES:(t + 1) * LANES], rhs, (((0,), (0,)), ((), ())),
                                preferred_element_type=F32)
            o = o[:, :LANES] / o[:, LANES:]
            obuf[pl.ds(q0, GRID_W), pair] = jnp.where(low, o[:GRID_W], o[GRID_W:]).astype(BF16)

    def stash(st):
        for i in range(2):
            sbuf[i * half:(i + 1) * half] = st[i]

    def one_row(rr, carry):
        geom = geometry(rr)
        following = geometry(jnp.minimum(rr + 1, rows_per_tile - 1))
        st = [sbuf[i * half:(i + 1) * half] for i in range(2)]
        for quad in range(NA_QUADS):
            if quad + 1 < NA_QUADS:
                st_next = scores(geom, quad + 1)
            else:
                stash(scores(following, 0))
            values(geom, quad, probs(st))
            st = st_next
        return carry

    stash(scores(geometry(0), 0))
    lax.fori_loop(0, rows_per_tile, one_row, 0, unroll=4)
    o_ref[...] = x_ref[...] + _dot(obuf[...], wo_ref[...])


def _mix1_attn(x, q, k, v, tab, w_out):
    bsz, seq, _ = x.shape
    tq = NA_TQ
    grid_rows = seq // GRID_W
    assert grid_rows >= NA_KH and seq % tq == 0
    tile = pl.BlockSpec((None, tq, D_MODEL), lambda b, i: (b, i, 0))
    whole = pl.BlockSpec((None, seq, NA_WIDTH), lambda b, i: (b, 0, 0),
                         pipeline_mode=pl.Buffered(1))
    return pl.pallas_call(
        functools.partial(_mix1_attn_kernel, grid_rows=grid_rows),
        grid=(bsz, seq // tq),
        in_specs=[tile, tile, whole, whole,
                  _const_spec((NA_QUADS, RPB_H * GRID_W, MXU_DIM)),
                  _const_spec((D_MODEL, D_MODEL))],
        out_specs=tile,
        out_shape=jax.ShapeDtypeStruct(x.shape, F32),
        scratch_shapes=[pltpu.VMEM((tq, NA_WIDTH), BF16),
                        pltpu.VMEM((NA_KH * GRID_W, MXU_DIM), F32)],
        compiler_params=_params("parallel", "arbitrary"),
        name="mix1_attn",
    )(x, q, k, v, tab, w_out)


def _prepare(l0_ffn1_norm, l0_ffn1_w_gate, l0_ffn1_w_up, l0_ffn1_w_down, l0_mix_norm, l0_w_in,
             l0_sgu_ln_g, l0_sgu_ln_b, l0_sgu_w, l0_sgu_b, l0_sink, l0_w_out, l0_ffn2_norm,
             l0_ffn2_w_gate, l0_ffn2_w_up, l0_ffn2_w_down, l1_ffn1_norm, l1_ffn1_w_gate,
             l1_ffn1_w_up, l1_ffn1_w_down, l1_mix_norm, l1_w_qkv, l1_rpb, l1_w_out, l1_ffn2_norm,
             l1_ffn2_w_gate, l1_ffn2_w_up, l1_ffn2_w_down, final_norm):
    def row(v):
        return v.reshape(1, -1)

    def ffn(n, g, u, d):
        return (row(n), g.astype(BF16), u.astype(BF16), d.astype(BF16))

    head_order = [ATT_REP * g + r for r in range(ATT_REP) for g in range(ATT_KV_HEADS)]
    q0 = 2 * SGU_WIDTH
    q_cols = jnp.concatenate([jnp.arange(HEAD_DIM) + q0 + h * HEAD_DIM for h in head_order])
    in_cols = jnp.concatenate([jnp.arange(q0), q_cols, jnp.arange(q0 + Q_WIDTH, AB_IN)])
    out_rows = jnp.concatenate([jnp.arange(SGU_WIDTH), q_cols - q0 + SGU_WIDTH])
    sink_col = jnp.repeat(l0_sink[jnp.array(head_order)], BLOCK).reshape(-1, 1)
    sgu_b = jnp.broadcast_to(l0_sgu_b[:, :, None], (SGU_GROUPS, BLOCK, LANES))
    return dict(
        l0_f1=ffn(l0_ffn1_norm, l0_ffn1_w_gate, l0_ffn1_w_up, l0_ffn1_w_down),
        l0_f2=ffn(l0_ffn2_norm, l0_ffn2_w_gate, l0_ffn2_w_up, l0_ffn2_w_down),
        l1_f1=ffn(l1_ffn1_norm, l1_ffn1_w_gate, l1_ffn1_w_up, l1_ffn1_w_down),
        l1_f2=ffn(l1_ffn2_norm, l1_ffn2_w_gate, l1_ffn2_w_up, l1_ffn2_w_down),
        l0_mix=(row(l0_mix_norm), l0_w_in[:, in_cols].astype(BF16), row(l0_sgu_ln_g),
                row(l0_sgu_ln_b), l0_sgu_w.astype(BF16), sgu_b),
        l0_sink=sink_col,
        l0_w_out=l0_w_out[out_rows].astype(BF16),
        l1_mix=(row(l1_mix_norm), l1_w_qkv.astype(BF16)),
        l1_rpb=l1_rpb,
        l1_w_out=l1_w_out.astype(BF16),
        final_norm=row(final_norm),
    )


def _trunk(x, w, cos, sin, tab):
    bsz, seq, _ = x.shape
    n = bsz * seq
    assert seq % ATT_TQ == 0 and seq % FFN_TM == 0
    fn = w["final_norm"]
    x = _ffn(x.reshape(n, D_MODEL), *w["l0_f1"], fn, final=False)
    a, q, k, v = _mix0_proj(x, *w["l0_mix"], cos, sin, seq)

    def seqs(t):
        return t.reshape(bsz, seq, t.shape[-1])

    x = _mix0_attn(seqs(x), seqs(a), seqs(q), seqs(k), seqs(v), w["l0_sink"], w["l0_w_out"])
    x = _ffn(x.reshape(n, D_MODEL), *w["l0_f2"], fn, final=False)
    x = _ffn(x, *w["l1_f1"], fn, final=False)
    q, k, v = _mix1_proj(x, *w["l1_mix"])
    x = _mix1_attn(seqs(x), seqs(q), seqs(k), seqs(v), tab, w["l1_w_out"])
    x = _ffn(x.reshape(n, D_MODEL), *w["l1_f2"], fn, final=True)
    return x.reshape(bsz, seq, D_MODEL)


def kernel(x_prompt, x_sample, l0_ffn1_norm, l0_ffn1_w_gate, l0_ffn1_w_up, l0_ffn1_w_down, l0_mix_norm, l0_w_in, l0_sgu_ln_g, l0_sgu_ln_b, l0_sgu_w, l0_sgu_b, l0_sink, l0_w_out, l0_ffn2_norm, l0_ffn2_w_gate, l0_ffn2_w_up, l0_ffn2_w_down, l1_ffn1_norm, l1_ffn1_w_gate, l1_ffn1_w_up, l1_ffn1_w_down, l1_mix_norm, l1_w_qkv, l1_rpb, l1_w_out, l1_ffn2_norm, l1_ffn2_w_gate, l1_ffn2_w_up, l1_ffn2_w_down, final_norm):
    w = _prepare(l0_ffn1_norm, l0_ffn1_w_gate, l0_ffn1_w_up, l0_ffn1_w_down, l0_mix_norm, l0_w_in,
                 l0_sgu_ln_g, l0_sgu_ln_b, l0_sgu_w, l0_sgu_b, l0_sink, l0_w_out, l0_ffn2_norm,
                 l0_ffn2_w_gate, l0_ffn2_w_up, l0_ffn2_w_down, l1_ffn1_norm, l1_ffn1_w_gate,
                 l1_ffn1_w_up, l1_ffn1_w_down, l1_mix_norm, l1_w_qkv, l1_rpb, l1_w_out,
                 l1_ffn2_norm, l1_ffn2_w_gate, l1_ffn2_w_up, l1_ffn2_w_down, final_norm)
    cos, sin = _rope_table(max(x_prompt.shape[1], x_sample.shape[1]))
    tab = _na_bias_table(l1_rpb)
    return (_trunk(x_prompt, w, cos, sin, tab), _trunk(x_sample, w, cos, sin, tab))
```

```python
import functools

import jax
import jax.numpy as jnp
from jax import lax
from jax.experimental import pallas as pl
from jax.experimental.pallas import tpu as pltpu

D_MODEL = 1024
D_FF = 2816
BLOCK = 128
SGU_WIDTH = 512
SGU_GROUPS = 4
HEAD_DIM = 64
ATT_HEADS = 8
ATT_KV_HEADS = 2
ATT_REP = ATT_HEADS // ATT_KV_HEADS
WINDOW = 128
ROPE_THETA = 500000.0
ROPE_DIM = HEAD_DIM // 4
ROPE_HALF = ROPE_DIM // 2
Q_WIDTH = ATT_HEADS * HEAD_DIM
KV_WIDTH = ATT_KV_HEADS * HEAD_DIM
AB_IN = 2 * SGU_WIDTH + Q_WIDTH + 2 * KV_WIDTH
NA_HEADS = 16
NA_WIDTH = NA_HEADS * HEAD_DIM
NA_KH = 8
NA_KW = 16
GRID_W = 64
EPS = 1e-6
NEG_INF = -1e30
SCALE = HEAD_DIM ** -0.5
SQRT_HALF = 0.5 ** 0.5
LOG2E = 1.4426950408889634

LANES = 128
MXU_DIM = 256
QUAD = MXU_DIM // HEAD_DIM
NA_QUADS = NA_HEADS // QUAD
RPB_H = 2 * NA_KH - 1
RPB_W = 2 * NA_KW - 1
RPB_CODE = 32

FFN_TM = 512
FFN_FC = 256
ATT_TQ = 1024
NA_TQ = 512
VMEM_LIMIT = 56 * 1024 * 1024

F32 = jnp.float32
BF16 = jnp.bfloat16


def _params(*sem):
    return pltpu.CompilerParams(dimension_semantics=sem, vmem_limit_bytes=VMEM_LIMIT)


def _const_spec(shape):
    nd = len(shape)
    return pl.BlockSpec(shape, lambda *_: (0,) * nd, pipeline_mode=pl.Buffered(1))


def _rms(x, g):
    return x * lax.rsqrt(jnp.mean(x * x, axis=-1, keepdims=True) + EPS) * g


def _dot(a, b):
    return jnp.dot(a, b, preferred_element_type=F32)


def _dot_nt(a, b):
    return lax.dot_general(a, b, (((1,), (1,)), ((), ())), preferred_element_type=F32)


def _low_head_lanes(shape):
    return lax.broadcasted_iota(jnp.int32, shape, len(shape) - 1) % LANES < HEAD_DIM


def _ffn_kernel(x_ref, n_ref, wg_ref, wu_ref, wd_ref, fn_ref, o_ref, a_ref, *, final):
    x = x_ref[...]
    h = _rms(x, n_ref[...]).astype(BF16)
    for j in range(D_FF // FFN_FC):
        sl = slice(j * FFN_FC, (j + 1) * FFN_FC)
        g = _dot(h, wg_ref[:, sl])
        u = _dot(h, wu_ref[:, sl])
        a_ref[:, sl] = (g * jax.nn.sigmoid(g) * u).astype(BF16)
    y = x + 0.5 * _dot(a_ref[...], wd_ref[...])
    if final:
        y = _rms(y, fn_ref[...])
    o_ref[...] = y


def _ffn(x, norm, wg, wu, wd, final_norm, *, final):
    n = x.shape[0]
    tile = pl.BlockSpec((FFN_TM, D_MODEL), lambda i: (i, 0))
    return pl.pallas_call(
        functools.partial(_ffn_kernel, final=final),
        grid=(n // FFN_TM,),
        in_specs=[tile, _const_spec((1, D_MODEL)), _const_spec((D_MODEL, D_FF)),
                  _const_spec((D_MODEL, D_FF)), _const_spec((D_FF, D_MODEL)),
                  _const_spec((1, D_MODEL))],
        out_specs=tile,
        out_shape=jax.ShapeDtypeStruct((n, D_MODEL), F32),
        scratch_shapes=[pltpu.VMEM((FFN_TM, D_FF), BF16)],
        compiler_params=_params("parallel"),
        name="ffn_final" if final else "ffn",
    )(x, norm, wg, wu, wd, final_norm)


def _rope_table_kernel(cos_ref, sin_ref):
    rows = cos_ref.shape[0]
    shape = (rows, LANES)
    lane = lax.broadcasted_iota(jnp.int32, shape, 1) % HEAD_DIM
    pos = (lax.broadcasted_iota(jnp.int32, shape, 0) + pl.program_id(0) * rows).astype(F32)
    expo = -(lane % ROPE_HALF).astype(F32) / ROPE_HALF
    inv = jnp.where(lane < ROPE_DIM, jnp.power(ROPE_THETA, expo), 0.0)
    ang = pos * inv
    s = jnp.sin(ang)
    cos_ref[...] = jnp.cos(ang)
    sin_ref[...] = jnp.where(lane < ROPE_HALF, -s, s)


def _rope_table(length):
    rows = 512
    spec = pl.BlockSpec((rows, LANES), lambda i: (i, 0))
    out = jax.ShapeDtypeStruct((length, LANES), F32)
    return pl.pallas_call(
        _rope_table_kernel, grid=(length // rows,), out_specs=[spec, spec], out_shape=[out, out],
        compiler_params=_params("parallel"), name="rope_table")()


def _rope(t, cos, sin):
    lane = lax.broadcasted_iota(jnp.int32, t.shape, 1) % HEAD_DIM
    partner = jnp.where(lane < ROPE_HALF,
                        pltpu.roll(t, LANES - ROPE_HALF, 1),
                        pltpu.roll(t, ROPE_HALF, 1))
    return t * cos + partner * sin


def _mix0_proj_kernel(x_ref, n_ref, win_ref, lng_ref, lnb_ref, sw_ref, sb_ref, cos_ref, sin_ref,
                      a_ref, q_ref, k_ref, v_ref):
    tm = x_ref.shape[0]
    h = _rms(x_ref[...], n_ref[...]).astype(BF16)
    z = _dot(h, win_ref[...])
    zg = z[:, :2 * SGU_WIDTH]
    zs = 0.5 * zg * (1.0 + lax.erf(zg * SQRT_HALF))
    u = zs[:, :SGU_WIDTH]
    vh = zs[:, SGU_WIDTH:]
    mu = jnp.mean(vh, axis=-1, keepdims=True)
    d = vh - mu
    var = jnp.mean(d * d, axis=-1, keepdims=True)
    vn = (d * lax.rsqrt(var + EPS) * lng_ref[...] + lnb_ref[...]).astype(BF16)
    for c in range(tm // BLOCK):
        rows = slice(c * BLOCK, (c + 1) * BLOCK)
        for g in range(SGU_GROUPS):
            cols = slice(g * LANES, (g + 1) * LANES)
            mix = _dot(sw_ref[g], vn[rows, cols]) + sb_ref[g]
            a_ref[rows, cols] = (u[rows, cols] * mix).astype(BF16)
    cos = cos_ref[...]
    sin = sin_ref[...]
    o = 2 * SGU_WIDTH
    for p in range(Q_WIDTH // LANES):
        t = z[:, o + p * LANES:o + (p + 1) * LANES]
        q_ref[:, p * LANES:(p + 1) * LANES] = (_rope(t, cos, sin) * (SCALE * LOG2E)).astype(BF16)
    o += Q_WIDTH
    k_ref[...] = _rope(z[:, o:o + KV_WIDTH], cos, sin).astype(BF16)
    o += KV_WIDTH
    v_ref[...] = z[:, o:o + KV_WIDTH].astype(BF16)


def _mix0_proj(x, norm, w_in, ln_g, ln_b, sgu_w, sgu_b, cos, sin, seq):
    n = x.shape[0]
    tm = FFN_TM
    per_seq = seq // tm

    def tile(w):
        return pl.BlockSpec((tm, w), lambda i: (i, 0))

    def out(w):
        return jax.ShapeDtypeStruct((n, w), BF16)

    pos = pl.BlockSpec((tm, LANES), lambda i: (i % per_seq, 0))
    return pl.pallas_call(
        _mix0_proj_kernel,
        grid=(n // tm,),
        in_specs=[tile(D_MODEL), _const_spec((1, D_MODEL)), _const_spec((D_MODEL, AB_IN)),
                  _const_spec((1, SGU_WIDTH)), _const_spec((1, SGU_WIDTH)),
                  _const_spec((SGU_GROUPS, BLOCK, BLOCK)), _const_spec((SGU_GROUPS, BLOCK, LANES)),
                  pos, pos],
        out_specs=[tile(SGU_WIDTH), tile(Q_WIDTH), tile(KV_WIDTH), tile(KV_WIDTH)],
        out_shape=[out(SGU_WIDTH), out(Q_WIDTH), out(KV_WIDTH), out(KV_WIDTH)],
        compiler_params=_params("parallel"),
        name="mix0_proj",
    )(x, norm, w_in, ln_g, ln_b, sgu_w, sgu_b, cos, sin)


def _mix0_attn_kernel(x_ref, a_ref, q_ref, kp_ref, kc_ref, kn_ref, vp_ref, vc_ref, vn_ref,
                      sink_ref, wo_ref, o_ref, kbuf, vbuf, bbuf, *, seq):
    tq = x_ref.shape[0]
    pos0 = pl.program_id(1) * tq
    kbuf[0:BLOCK] = kp_ref[...]
    kbuf[BLOCK:BLOCK + tq] = kc_ref[...]
    kbuf[BLOCK + tq:] = kn_ref[...]
    vbuf[0:BLOCK] = vp_ref[...]
    vbuf[BLOCK:BLOCK + tq] = vc_ref[...]
    vbuf[BLOCK + tq:] = vn_ref[...]
    nslab = ATT_REP * ATT_KV_HEADS
    kw_len = 3 * BLOCK
    low = _low_head_lanes((BLOCK, LANES))
    qi = lax.broadcasted_iota(jnp.int32, (BLOCK, kw_len), 0)
    kj = lax.broadcasted_iota(jnp.int32, (BLOCK, kw_len), 1)
    in_band = jnp.abs(kj - BLOCK - qi) <= WINDOW
    sink = sink_ref[...] * LOG2E
    ones = jnp.ones((kw_len, LANES), BF16)
    half = nslab * BLOCK // 2

    def scores(b):
        rows = slice(b * BLOCK, (b + 1) * BLOCK)
        kpos = pos0 + (b - 1) * BLOCK + kj
        valid = in_band & (kpos >= 0) & (kpos < seq)
        slabs = []
        for r in range(ATT_REP):
            qp = q_ref[rows, r * LANES:(r + 1) * LANES]
            slabs.append(jnp.where(low, qp, jnp.zeros_like(qp)))
            slabs.append(jnp.where(low, jnp.zeros_like(qp), qp))
        s = _dot_nt(jnp.concatenate(slabs, axis=0), kbuf[b * BLOCK:b * BLOCK + kw_len])
        s = jnp.where(valid[None], s.reshape(nslab, BLOCK, kw_len), NEG_INF)
        return s.reshape(nslab * BLOCK, kw_len)

    def finish(b, s):
        rows = slice(b * BLOCK, (b + 1) * BLOCK)
        m = jnp.maximum(jnp.max(s, axis=-1, keepdims=True), sink)
        e = jnp.exp2(s - m).astype(BF16)
        rhs = jnp.concatenate([vbuf[b * BLOCK:b * BLOCK + kw_len], ones], axis=1)
        o = jnp.concatenate([_dot(e[i * half:(i + 1) * half], rhs) for i in range(2)], axis=0)
        o = o[:, :LANES] / (o[:, LANES:] + jnp.exp2(sink - m))
        for r in range(ATT_REP):
            o_lo = o[(2 * r) * BLOCK:(2 * r + 1) * BLOCK]
            o_hi = o[(2 * r + 1) * BLOCK:(2 * r + 2) * BLOCK]
            bbuf[rows, r * LANES:(r + 1) * LANES] = jnp.where(low, o_lo, o_hi).astype(BF16)

    nblk = tq // BLOCK
    s = scores(0)
    for b in range(nblk):
        s_next = scores(b + 1) if b + 1 < nblk else None
        finish(b, s)
        s = s_next
    y = _dot(a_ref[...], wo_ref[:SGU_WIDTH]) + _dot(bbuf[...], wo_ref[SGU_WIDTH:])
    o_ref[...] = x_ref[...] + y


def _mix0_attn(x, a, q, k, v, sink_col, w_out):
    bsz, seq, _ = x.shape
    tq = ATT_TQ
    per = tq // BLOCK
    nblk = seq // BLOCK

    def tile(w):
        return pl.BlockSpec((None, tq, w), lambda b, i: (b, i, 0))

    prev = pl.BlockSpec((None, BLOCK, KV_WIDTH), lambda b, i: (b, jnp.maximum(i * per - 1, 0), 0))
    nxt = pl.BlockSpec((None, BLOCK, KV_WIDTH),
                       lambda b, i: (b, jnp.minimum((i + 1) * per, nblk - 1), 0))
    return pl.pallas_call(
        functools.partial(_mix0_attn_kernel, seq=seq),
        grid=(bsz, seq // tq),
        in_specs=[tile(D_MODEL), tile(SGU_WIDTH), tile(Q_WIDTH),
                  prev, tile(KV_WIDTH), nxt, prev, tile(KV_WIDTH), nxt,
                  _const_spec((ATT_HEADS * BLOCK, 1)), _const_spec((D_MODEL, D_MODEL))],
        out_specs=tile(D_MODEL),
        out_shape=jax.ShapeDtypeStruct(x.shape, F32),
        scratch_shapes=[pltpu.VMEM((tq + 2 * BLOCK, KV_WIDTH), BF16),
                        pltpu.VMEM((tq + 2 * BLOCK, KV_WIDTH), BF16),
                        pltpu.VMEM((tq, Q_WIDTH), BF16)],
        compiler_params=_params("parallel", "parallel"),
        name="mix0_attn",
    )(x, a, q, k, k, k, v, v, v, sink_col, w_out)


def _mix1_proj_kernel(x_ref, n_ref, w_ref, q_ref, k_ref, v_ref):
    h = _rms(x_ref[...], n_ref[...]).astype(BF16)
    q_ref[...] = (_dot(h, w_ref[:, :NA_WIDTH]) * (SCALE * LOG2E)).astype(BF16)
    k_ref[...] = _dot(h, w_ref[:, NA_WIDTH:2 * NA_WIDTH]).astype(BF16)
    v_ref[...] = _dot(h, w_ref[:, 2 * NA_WIDTH:]).astype(BF16)


def _mix1_proj(x, norm, w_qkv):
    n = x.shape[0]
    tile = pl.BlockSpec((FFN_TM, D_MODEL), lambda i: (i, 0))
    out = jax.ShapeDtypeStruct((n, NA_WIDTH), BF16)
    return pl.pallas_call(
        _mix1_proj_kernel,
        grid=(n // FFN_TM,),
        in_specs=[tile, _const_spec((1, D_MODEL)), _const_spec((D_MODEL, 3 * NA_WIDTH))],
        out_specs=[tile, tile, tile],
        out_shape=[out, out, out],
        compiler_params=_params("parallel"),
        name="mix1_proj",
    )(x, norm, w_qkv)


def _na_bias_kernel(rpb_ref, tab_ref):
    quad = pl.program_id(0)
    shape = (GRID_W, MXU_DIM)
    kc = lax.broadcasted_iota(jnp.int32, shape, 0)
    lane = lax.broadcasted_iota(jnp.int32, shape, 1)
    c = lane % GRID_W
    start = jnp.clip(c - NA_KW // 2, 0, GRID_W - NA_KW)
    in_win = (kc >= start) & (kc < start + NA_KW)
    code = jnp.where(in_win, (lane // GRID_W) * RPB_CODE + (kc - c + NA_KW - 1), -1)
    for dr in range(RPB_H):
        val = jnp.full(shape, NEG_INF, F32)
        for h in range(QUAD):
            base = ((quad * QUAD + h) * RPB_H + dr) * RPB_W
            for dcol in range(RPB_W):
                val = jnp.where(code == h * RPB_CODE + dcol, rpb_ref[base + dcol] * LOG2E, val)
        tab_ref[dr * GRID_W:(dr + 1) * GRID_W, :] = val


def _na_bias_table(rpb):
    return pl.pallas_call(
        _na_bias_kernel,
        grid=(NA_QUADS,),
        in_specs=[pl.BlockSpec(memory_space=pltpu.SMEM)],
        out_specs=pl.BlockSpec((None, RPB_H * GRID_W, MXU_DIM), lambda p: (p, 0, 0)),
        out_shape=jax.ShapeDtypeStruct((NA_QUADS, RPB_H * GRID_W, MXU_DIM), F32),
        compiler_params=_params("parallel"),
        name="na_bias_table",
    )(rpb.reshape(-1))


def _mix1_attn_kernel(x_ref, q_ref, k_ref, v_ref, tab_ref, wo_ref, o_ref, obuf, sbuf, *, grid_rows):
    tq = x_ref.shape[0]
    rows_per_tile = tq // GRID_W
    band = NA_KH * GRID_W
    row0 = pl.program_id(1) * rows_per_tile
    lane_head = lax.broadcasted_iota(jnp.int32, (GRID_W, MXU_DIM), 1) // HEAD_DIM
    low = _low_head_lanes((GRID_W, LANES))
    ones = jnp.ones((band, LANES), BF16)

    half = band // 2

    def geometry(rr):
        r = row0 + rr
        rs = jnp.clip(r - NA_KH // 2, 0, grid_rows - NA_KH)
        return (pl.multiple_of(rs * GRID_W, GRID_W), pl.multiple_of(rr * GRID_W, GRID_W),
                pl.multiple_of((rs - r + NA_KH - 1) * GRID_W, GRID_W))

    def scores(geom, quad):
        k0, q0, t0 = geom
        cols = slice(quad * MXU_DIM, (quad + 1) * MXU_DIM)
        qq = q_ref[pl.ds(q0, GRID_W), cols]
        qs = jnp.concatenate([jnp.where(lane_head == h, qq, jnp.zeros_like(qq))
                              for h in range(QUAD)], axis=0)
        return [_dot_nt(k_ref[pl.ds(k0 + i * half, half), cols], qs)
                + tab_ref[quad, pl.ds(t0 + i * half, half), :] for i in range(2)]

    def probs(st):
        st = jnp.concatenate(st, axis=0)
        return jnp.exp2(st - jnp.max(st, axis=0, keepdims=True)).astype(BF16)

    def values(geom, quad, e):
        k0, q0, _ = geom
        for t in range(QUAD // 2):
            pair = slice(quad * MXU_DIM + t * LANES, quad * MXU_DIM + (t + 1) * LANES)
            rhs = jnp.concatenate([v_ref[pl.ds(k0, band), pair], ones], axis=1)
            o = lax.dot_general(e[:, t * LANES:(t + 1) * LANES], rhs, (((0,), (0,)), ((), ())),
                                preferred_element_type=F32)
            o = o[:, :LANES] / o[:, LANES:]
            obuf[pl.ds(q0, GRID_W), pair] = jnp.where(low, o[:GRID_W], o[GRID_W:]).astype(BF16)

    def stash(st):
        for i in range(2):
            sbuf[i * half:(i + 1) * half] = st[i]

    def one_row(rr, carry):
        geom = geometry(rr)
        following = geometry(jnp.minimum(rr + 1, rows_per_tile - 1))
        st = [sbuf[i * half:(i + 1) * half] for i in range(2)]
        for quad in range(NA_QUADS):
            if quad + 1 < NA_QUADS:
                st_next = scores(geom, quad + 1)
            else:
                stash(scores(following, 0))
            values(geom, quad, probs(st))
            st = st_next
        return carry

    stash(scores(geometry(0), 0))
    lax.fori_loop(0, rows_per_tile, one_row, 0, unroll=4)
    o_ref[...] = x_ref[...] + _dot(obuf[...], wo_ref[...])


def _mix1_attn(x, q, k, v, tab, w_out):
    bsz, seq, _ = x.shape
    tq = NA_TQ
    grid_rows = seq // GRID_W
    assert grid_rows >= NA_KH and seq % tq == 0
    tile = pl.BlockSpec((None, tq, D_MODEL), lambda b, i: (b, i, 0))
    whole = pl.BlockSpec((None, seq, NA_WIDTH), lambda b, i: (b, 0, 0),
                         pipeline_mode=pl.Buffered(1))
    return pl.pallas_call(
        functools.partial(_mix1_attn_kernel, grid_rows=grid_rows),
        grid=(bsz, seq // tq),
        in_specs=[tile, tile, whole, whole,
                  _const_spec((NA_QUADS, RPB_H * GRID_W, MXU_DIM)),
                  _const_spec((D_MODEL, D_MODEL))],
        out_specs=tile,
        out_shape=jax.ShapeDtypeStruct(x.shape, F32),
        scratch_shapes=[pltpu.VMEM((tq, NA_WIDTH), BF16),
                        pltpu.VMEM((NA_KH * GRID_W, MXU_DIM), F32)],
        compiler_params=_params("parallel", "arbitrary"),
        name="mix1_attn",
    )(x, q, k, v, tab, w_out)


def _prepare(l0_ffn1_norm, l0_ffn1_w_gate, l0_ffn1_w_up, l0_ffn1_w_down, l0_mix_norm, l0_w_in,
             l0_sgu_ln_g, l0_sgu_ln_b, l0_sgu_w, l0_sgu_b, l0_sink, l0_w_out, l0_ffn2_norm,
             l0_ffn2_w_gate, l0_ffn2_w_up, l0_ffn2_w_down, l1_ffn1_norm, l1_ffn1_w_gate,
             l1_ffn1_w_up, l1_ffn1_w_down, l1_mix_norm, l1_w_qkv, l1_rpb, l1_w_out, l1_ffn2_norm,
             l1_ffn2_w_gate, l1_ffn2_w_up, l1_ffn2_w_down, final_norm):
    def row(v):
        return v.reshape(1, -1)

    def ffn(n, g, u, d):
        return (row(n), g.astype(BF16), u.astype(BF16), d.astype(BF16))

    head_order = [ATT_REP * g + r for r in range(ATT_REP) for g in range(ATT_KV_HEADS)]
    q0 = 2 * SGU_WIDTH
    q_cols = jnp.concatenate([jnp.arange(HEAD_DIM) + q0 + h * HEAD_DIM for h in head_order])
    in_cols = jnp.concatenate([jnp.arange(q0), q_cols, jnp.arange(q0 + Q_WIDTH, AB_IN)])
    out_rows = jnp.concatenate([jnp.arange(SGU_WIDTH), q_cols - q0 + SGU_WIDTH])
    sink_col = jnp.repeat(l0_sink[jnp.array(head_order)], BLOCK).reshape(-1, 1)
    sgu_b = jnp.broadcast_to(l0_sgu_b[:, :, None], (SGU_GROUPS, BLOCK, LANES))
    return dict(
        l0_f1=ffn(l0_ffn1_norm, l0_ffn1_w_gate, l0_ffn1_w_up, l0_ffn1_w_down),
        l0_f2=ffn(l0_ffn2_norm, l0_ffn2_w_gate, l0_ffn2_w_up, l0_ffn2_w_down),
        l1_f1=ffn(l1_ffn1_norm, l1_ffn1_w_gate, l1_ffn1_w_up, l1_ffn1_w_down),
        l1_f2=ffn(l1_ffn2_norm, l1_ffn2_w_gate, l1_ffn2_w_up, l1_ffn2_w_down),
        l0_mix=(row(l0_mix_norm), l0_w_in[:, in_cols].astype(BF16), row(l0_sgu_ln_g),
                row(l0_sgu_ln_b), l0_sgu_w.astype(BF16), sgu_b),
        l0_sink=sink_col,
        l0_w_out=l0_w_out[out_rows].astype(BF16),
        l1_mix=(row(l1_mix_norm), l1_w_qkv.astype(BF16)),
        l1_rpb=l1_rpb,
        l1_w_out=l1_w_out.astype(BF16),
        final_norm=row(final_norm),
    )


def _trunk(x, w, cos, sin, tab):
    bsz, seq, _ = x.shape
    n = bsz * seq
    assert seq % ATT_TQ == 0 and seq % FFN_TM == 0
    fn = w["final_norm"]
    x = _ffn(x.reshape(n, D_MODEL), *w["l0_f1"], fn, final=False)
    a, q, k, v = _mix0_proj(x, *w["l0_mix"], cos, sin, seq)

    def seqs(t):
        return t.reshape(bsz, seq, t.shape[-1])

    x = _mix0_attn(seqs(x), seqs(a), seqs(q), seqs(k), seqs(v), w["l0_sink"], w["l0_w_out"])
    x = _ffn(x.reshape(n, D_MODEL), *w["l0_f2"], fn, final=False)
    x = _ffn(x, *w["l1_f1"], fn, final=False)
    q, k, v = _mix1_proj(x, *w["l1_mix"])
    x = _mix1_attn(seqs(x), seqs(q), seqs(k), seqs(v), tab, w["l1_w_out"])
    x = _ffn(x.reshape(n, D_MODEL), *w["l1_f2"], fn, final=True)
    return x.reshape(bsz, seq, D_MODEL)


def kernel(x_prompt, x_sample, l0_ffn1_norm, l0_ffn1_w_gate, l0_ffn1_w_up, l0_ffn1_w_down, l0_mix_norm, l0_w_in, l0_sgu_ln_g, l0_sgu_ln_b, l0_sgu_w, l0_sgu_b, l0_sink, l0_w_out, l0_ffn2_norm, l0_ffn2_w_gate, l0_ffn2_w_up, l0_ffn2_w_down, l1_ffn1_norm, l1_ffn1_w_gate, l1_ffn1_w_up, l1_ffn1_w_down, l1_mix_norm, l1_w_qkv, l1_rpb, l1_w_out, l1_ffn2_norm, l1_ffn2_w_gate, l1_ffn2_w_up, l1_ffn2_w_down, final_norm):
    w = _prepare(l0_ffn1_norm, l0_ffn1_w_gate, l0_ffn1_w_up, l0_ffn1_w_down, l0_mix_norm, l0_w_in,
                 l0_sgu_ln_g, l0_sgu_ln_b, l0_sgu_w, l0_sgu_b, l0_sink, l0_w_out, l0_ffn2_norm,
                 l0_ffn2_w_gate, l0_ffn2_w_up, l0_ffn2_w_down, l1_ffn1_norm, l1_ffn1_w_gate,
                 l1_ffn1_w_up, l1_ffn1_w_down, l1_mix_norm, l1_w_qkv, l1_rpb, l1_w_out,
                 l1_ffn2_norm, l1_ffn2_w_gate, l1_ffn2_w_up, l1_ffn2_w_down, final_norm)
    cos, sin = _rope_table(max(x_prompt.shape[1], x_sample.shape[1]))
    tab = _na_bias_table(l1_rpb)
    return (_trunk(x_prompt, w, cos, sin, tab), _trunk(x_sample, w, cos, sin, tab))
```

```python
import functools

import jax
import jax.numpy as jnp
from jax import lax
from jax.experimental import pallas as pl
from jax.experimental.pallas import tpu as pltpu

D_MODEL = 1024
D_FF = 2816
BLOCK = 128
SGU_WIDTH = 512
SGU_GROUPS = 4
HEAD_DIM = 64
ATT_HEADS = 8
ATT_KV_HEADS = 2
ATT_REP = ATT_HEADS // ATT_KV_HEADS
WINDOW = 128
ROPE_THETA = 500000.0
ROPE_DIM = HEAD_DIM // 4
ROPE_HALF = ROPE_DIM // 2
Q_WIDTH = ATT_HEADS * HEAD_DIM
KV_WIDTH = ATT_KV_HEADS * HEAD_DIM
AB_IN = 2 * SGU_WIDTH + Q_WIDTH + 2 * KV_WIDTH
NA_HEADS = 16
NA_WIDTH = NA_HEADS * HEAD_DIM
NA_KH = 8
NA_KW = 16
GRID_W = 64
EPS = 1e-6
NEG_INF = -1e30
SCALE = HEAD_DIM ** -0.5
SQRT_HALF = 0.5 ** 0.5
LOG2E = 1.4426950408889634

LANES = 128
MXU_DIM = 256
QUAD = MXU_DIM // HEAD_DIM
NA_QUADS = NA_HEADS // QUAD
RPB_H = 2 * NA_KH - 1
RPB_W = 2 * NA_KW - 1
RPB_CODE = 32

FFN_TM = 512
FFN_FC = 256
ATT_TQ = 1024
NA_TQ = 1024
VMEM_LIMIT = 56 * 1024 * 1024

F32 = jnp.float32
BF16 = jnp.bfloat16


def _params(*sem):
    return pltpu.CompilerParams(dimension_semantics=sem, vmem_limit_bytes=VMEM_LIMIT)


def _const_spec(shape):
    nd = len(shape)
    return pl.BlockSpec(shape, lambda *_: (0,) * nd, pipeline_mode=pl.Buffered(1))


def _rms(x, g):
    return x * lax.rsqrt(jnp.mean(x * x, axis=-1, keepdims=True) + EPS) * g


def _dot(a, b):
    return jnp.dot(a, b, preferred_element_type=F32)


def _dot_nt(a, b):
    return lax.dot_general(a, b, (((1,), (1,)), ((), ())), preferred_element_type=F32)


def _low_head_lanes(shape):
    return lax.broadcasted_iota(jnp.int32, shape, len(shape) - 1) % LANES < HEAD_DIM


def _ffn_kernel(x_ref, n_ref, wg_ref, wu_ref, wd_ref, fn_ref, o_ref, a_ref, *, final):
    x = x_ref[...]
    h = _rms(x, n_ref[...]).astype(BF16)
    for j in range(D_FF // FFN_FC):
        sl = slice(j * FFN_FC, (j + 1) * FFN_FC)
        g = _dot(h, wg_ref[:, sl])
        u = _dot(h, wu_ref[:, sl])
        a_ref[:, sl] = (g * jax.nn.sigmoid(g) * u).astype(BF16)
    y = x + 0.5 * _dot(a_ref[...], wd_ref[...])
    if final:
        y = _rms(y, fn_ref[...])
    o_ref[...] = y


def _ffn(x, norm, wg, wu, wd, final_norm, *, final):
    n = x.shape[0]
    tile = pl.BlockSpec((FFN_TM, D_MODEL), lambda i: (i, 0))
    return pl.pallas_call(
        functools.partial(_ffn_kernel, final=final),
        grid=(n // FFN_TM,),
        in_specs=[tile, _const_spec((1, D_MODEL)), _const_spec((D_MODEL, D_FF)),
                  _const_spec((D_MODEL, D_FF)), _const_spec((D_FF, D_MODEL)),
                  _const_spec((1, D_MODEL))],
        out_specs=tile,
        out_shape=jax.ShapeDtypeStruct((n, D_MODEL), F32),
        scratch_shapes=[pltpu.VMEM((FFN_TM, D_FF), BF16)],
        compiler_params=_params("parallel"),
        name="ffn_final" if final else "ffn",
    )(x, norm, wg, wu, wd, final_norm)


def _rope_table_kernel(cos_ref, sin_ref):
    rows = cos_ref.shape[0]
    shape = (rows, LANES)
    lane = lax.broadcasted_iota(jnp.int32, shape, 1) % HEAD_DIM
    pos = (lax.broadcasted_iota(jnp.int32, shape, 0) + pl.program_id(0) * rows).astype(F32)
    expo = -(lane % ROPE_HALF).astype(F32) / ROPE_HALF
    inv = jnp.where(lane < ROPE_DIM, jnp.power(ROPE_THETA, expo), 0.0)
    ang = pos * inv
    s = jnp.sin(ang)
    cos_ref[...] = jnp.cos(ang)
    sin_ref[...] = jnp.where(lane < ROPE_HALF, -s, s)


def _rope_table(length):
    rows = 512
    spec = pl.BlockSpec((rows, LANES), lambda i: (i, 0))
    out = jax.ShapeDtypeStruct((length, LANES), F32)
    return pl.pallas_call(
        _rope_table_kernel, grid=(length // rows,), out_specs=[spec, spec], out_shape=[out, out],
        compiler_params=_params("parallel"), name="rope_table")()


def _rope(t, cos, sin):
    lane = lax.broadcasted_iota(jnp.int32, t.shape, 1) % HEAD_DIM
    partner = jnp.where(lane < ROPE_HALF,
                        pltpu.roll(t, LANES - ROPE_HALF, 1),
                        pltpu.roll(t, ROPE_HALF, 1))
    return t * cos + partner * sin


def _mix0_proj_kernel(x_ref, n_ref, win_ref, lng_ref, lnb_ref, sw_ref, sb_ref, cos_ref, sin_ref,
                      a_ref, q_ref, k_ref, v_ref):
    tm = x_ref.shape[0]
    h = _rms(x_ref[...], n_ref[...]).astype(BF16)
    z = _dot(h, win_ref[...])
    zg = z[:, :2 * SGU_WIDTH]
    zs = 0.5 * zg * (1.0 + lax.erf(zg * SQRT_HALF))
    u = zs[:, :SGU_WIDTH]
    vh = zs[:, SGU_WIDTH:]
    mu = jnp.mean(vh, axis=-1, keepdims=True)
    d = vh - mu
    var = jnp.mean(d * d, axis=-1, keepdims=True)
    vn = (d * lax.rsqrt(var + EPS) * lng_ref[...] + lnb_ref[...]).astype(BF16)
    for c in range(tm // BLOCK):
        rows = slice(c * BLOCK, (c + 1) * BLOCK)
        for g in range(SGU_GROUPS):
            cols = slice(g * LANES, (g + 1) * LANES)
            mix = _dot(sw_ref[g], vn[rows, cols]) + sb_ref[g]
            a_ref[rows, cols] = (u[rows, cols] * mix).astype(BF16)
    cos = cos_ref[...]
    sin = sin_ref[...]
    o = 2 * SGU_WIDTH
    for p in range(Q_WIDTH // LANES):
        t = z[:, o + p * LANES:o + (p + 1) * LANES]
        q_ref[:, p * LANES:(p + 1) * LANES] = (_rope(t, cos, sin) * (SCALE * LOG2E)).astype(BF16)
    o += Q_WIDTH
    k_ref[...] = _rope(z[:, o:o + KV_WIDTH], cos, sin).astype(BF16)
    o += KV_WIDTH
    v_ref[...] = z[:, o:o + KV_WIDTH].astype(BF16)


def _mix0_proj(x, norm, w_in, ln_g, ln_b, sgu_w, sgu_b, cos, sin, seq):
    n = x.shape[0]
    tm = FFN_TM
    per_seq = seq // tm

    def tile(w):
        return pl.BlockSpec((tm, w), lambda i: (i, 0))

    def out(w):
        return jax.ShapeDtypeStruct((n, w), BF16)

    pos = pl.BlockSpec((tm, LANES), lambda i: (i % per_seq, 0))
    return pl.pallas_call(
        _mix0_proj_kernel,
        grid=(n // tm,),
        in_specs=[tile(D_MODEL), _const_spec((1, D_MODEL)), _const_spec((D_MODEL, AB_IN)),
                  _const_spec((1, SGU_WIDTH)), _const_spec((1, SGU_WIDTH)),
                  _const_spec((SGU_GROUPS, BLOCK, BLOCK)), _const_spec((SGU_GROUPS, BLOCK, LANES)),
                  pos, pos],
        out_specs=[tile(SGU_WIDTH), tile(Q_WIDTH), tile(KV_WIDTH), tile(KV_WIDTH)],
        out_shape=[out(SGU_WIDTH), out(Q_WIDTH), out(KV_WIDTH), out(KV_WIDTH)],
        compiler_params=_params("parallel"),
        name="mix0_proj",
    )(x, norm, w_in, ln_g, ln_b, sgu_w, sgu_b, cos, sin)


def _mix0_attn_kernel(x_ref, a_ref, q_ref, kp_ref, kc_ref, kn_ref, vp_ref, vc_ref, vn_ref,
                      sink_ref, wo_ref, o_ref, kbuf, vbuf, bbuf, *, seq):
    tq = x_ref.shape[0]
    pos0 = pl.program_id(1) * tq
    kbuf[0:BLOCK] = kp_ref[...]
    kbuf[BLOCK:BLOCK + tq] = kc_ref[...]
    kbuf[BLOCK + tq:] = kn_ref[...]
    vbuf[0:BLOCK] = vp_ref[...]
    vbuf[BLOCK:BLOCK + tq] = vc_ref[...]
    vbuf[BLOCK + tq:] = vn_ref[...]
    nslab = ATT_REP * ATT_KV_HEADS
    kw_len = 3 * BLOCK
    low = _low_head_lanes((BLOCK, LANES))
    qi = lax.broadcasted_iota(jnp.int32, (BLOCK, kw_len), 0)
    kj = lax.broadcasted_iota(jnp.int32, (BLOCK, kw_len), 1)
    in_band = jnp.abs(kj - BLOCK - qi) <= WINDOW
    sink = sink_ref[...] * LOG2E
    ones = jnp.ones((kw_len, LANES), BF16)
    half = nslab * BLOCK // 2

    def scores(b):
        rows = slice(b * BLOCK, (b + 1) * BLOCK)
        kpos = pos0 + (b - 1) * BLOCK + kj
        valid = in_band & (kpos >= 0) & (kpos < seq)
        slabs = []
        for r in range(ATT_REP):
            qp = q_ref[rows, r * LANES:(r + 1) * LANES]
            slabs.append(jnp.where(low, qp, jnp.zeros_like(qp)))
            slabs.append(jnp.where(low, jnp.zeros_like(qp), qp))
        s = _dot_nt(jnp.concatenate(slabs, axis=0), kbuf[b * BLOCK:b * BLOCK + kw_len])
        s = jnp.where(valid[None], s.reshape(nslab, BLOCK, kw_len), NEG_INF)
        return s.reshape(nslab * BLOCK, kw_len)

    def finish(b, s):
        rows = slice(b * BLOCK, (b + 1) * BLOCK)
        m = jnp.maximum(jnp.max(s, axis=-1, keepdims=True), sink)
        e = jnp.exp2(s - m).astype(BF16)
        rhs = jnp.concatenate([vbuf[b * BLOCK:b * BLOCK + kw_len], ones], axis=1)
        o = jnp.concatenate([_dot(e[i * half:(i + 1) * half], rhs) for i in range(2)], axis=0)
        o = o[:, :LANES] / (o[:, LANES:] + jnp.exp2(sink - m))
        for r in range(ATT_REP):
            o_lo = o[(2 * r) * BLOCK:(2 * r + 1) * BLOCK]
            o_hi = o[(2 * r + 1) * BLOCK:(2 * r + 2) * BLOCK]
            bbuf[rows, r * LANES:(r + 1) * LANES] = jnp.where(low, o_lo, o_hi).astype(BF16)

    nblk = tq // BLOCK
    s = scores(0)
    for b in range(nblk):
        s_next = scores(b + 1) if b + 1 < nblk else None
        finish(b, s)
        s = s_next
    y = _dot(a_ref[...], wo_ref[:SGU_WIDTH]) + _dot(bbuf[...], wo_ref[SGU_WIDTH:])
    o_ref[...] = x_ref[...] + y


def _mix0_attn(x, a, q, k, v, sink_col, w_out):
    bsz, seq, _ = x.shape
    tq = ATT_TQ
    per = tq // BLOCK
    nblk = seq // BLOCK

    def tile(w):
        return pl.BlockSpec((None, tq, w), lambda b, i: (b, i, 0))

    prev = pl.BlockSpec((None, BLOCK, KV_WIDTH), lambda b, i: (b, jnp.maximum(i * per - 1, 0), 0))
    nxt = pl.BlockSpec((None, BLOCK, KV_WIDTH),
                       lambda b, i: (b, jnp.minimum((i + 1) * per, nblk - 1), 0))
    return pl.pallas_call(
        functools.partial(_mix0_attn_kernel, seq=seq),
        grid=(bsz, seq // tq),
        in_specs=[tile(D_MODEL), tile(SGU_WIDTH), tile(Q_WIDTH),
                  prev, tile(KV_WIDTH), nxt, prev, tile(KV_WIDTH), nxt,
                  _const_spec((ATT_HEADS * BLOCK, 1)), _const_spec((D_MODEL, D_MODEL))],
        out_specs=tile(D_MODEL),
        out_shape=jax.ShapeDtypeStruct(x.shape, F32),
        scratch_shapes=[pltpu.VMEM((tq + 2 * BLOCK, KV_WIDTH), BF16),
                        pltpu.VMEM((tq + 2 * BLOCK, KV_WIDTH), BF16),
                        pltpu.VMEM((tq, Q_WIDTH), BF16)],
        compiler_params=_params("parallel", "parallel"),
        name="mix0_attn",
    )(x, a, q, k, k, k, v, v, v, sink_col, w_out)


def _mix1_proj_kernel(x_ref, n_ref, w_ref, q_ref, k_ref, v_ref):
    h = _rms(x_ref[...], n_ref[...]).astype(BF16)
    q_ref[...] = (_dot(h, w_ref[:, :NA_WIDTH]) * (SCALE * LOG2E)).astype(BF16)
    k_ref[...] = _dot(h, w_ref[:, NA_WIDTH:2 * NA_WIDTH]).astype(BF16)
    v_ref[...] = _dot(h, w_ref[:, 2 * NA_WIDTH:]).astype(BF16)


def _mix1_proj(x, norm, w_qkv):
    n = x.shape[0]
    tile = pl.BlockSpec((FFN_TM, D_MODEL), lambda i: (i, 0))
    out = jax.ShapeDtypeStruct((n, NA_WIDTH), BF16)
    return pl.pallas_call(
        _mix1_proj_kernel,
        grid=(n // FFN_TM,),
        in_specs=[tile, _const_spec((1, D_MODEL)), _const_spec((D_MODEL, 3 * NA_WIDTH))],
        out_specs=[tile, tile, tile],
        out_shape=[out, out, out],
        compiler_params=_params("parallel"),
        name="mix1_proj",
    )(x, norm, w_qkv)


def _na_bias_kernel(rpb_ref, tab_ref):
    quad = pl.program_id(0)
    shape = (GRID_W, MXU_DIM)
    kc = lax.broadcasted_iota(jnp.int32, shape, 0)
    lane = lax.broadcasted_iota(jnp.int32, shape, 1)
    c = lane % GRID_W
    start = jnp.clip(c - NA_KW // 2, 0, GRID_W - NA_KW)
    in_win = (kc >= start) & (kc < start + NA_KW)
    code = jnp.where(in_win, (lane // GRID_W) * RPB_CODE + (kc - c + NA_KW - 1), -1)
    for dr in range(RPB_H):
        val = jnp.full(shape, NEG_INF, F32)
        for h in range(QUAD):
            base = ((quad * QUAD + h) * RPB_H + dr) * RPB_W
            for dcol in range(RPB_W):
                val = jnp.where(code == h * RPB_CODE + dcol, rpb_ref[base + dcol] * LOG2E, val)
        tab_ref[dr * GRID_W:(dr + 1) * GRID_W, :] = val


def _na_bias_table(rpb):
    return pl.pallas_call(
        _na_bias_kernel,
        grid=(NA_QUADS,),
        in_specs=[pl.BlockSpec(memory_space=pltpu.SMEM)],
        out_specs=pl.BlockSpec((None, RPB_H * GRID_W, MXU_DIM), lambda p: (p, 0, 0)),
        out_shape=jax.ShapeDtypeStruct((NA_QUADS, RPB_H * GRID_W, MXU_DIM), F32),
        compiler_params=_params("parallel"),
        name="na_bias_table",
    )(rpb.reshape(-1))


def _mix1_attn_kernel(x_ref, q_ref, k_ref, v_ref, tab_ref, wo_ref, o_ref, obuf, sbuf, *, grid_rows):
    tq = x_ref.shape[0]
    rows_per_tile = tq // GRID_W
    band = NA_KH * GRID_W
    row0 = pl.program_id(1) * rows_per_tile
    lane_head = lax.broadcasted_iota(jnp.int32, (GRID_W, MXU_DIM), 1) // HEAD_DIM
    low = _low_head_lanes((GRID_W, LANES))
    ones = jnp.ones((band, LANES), BF16)

    def geometry(rr):
        r = row0 + rr
        rs = jnp.clip(r - NA_KH // 2, 0, grid_rows - NA_KH)
        return (pl.multiple_of(rs * GRID_W, GRID_W), pl.multiple_of(rr * GRID_W, GRID_W),
                pl.multiple_of((rs - r + NA_KH - 1) * GRID_W, GRID_W))

    def scores(geom, quad):
        k0, q0, t0 = geom
        cols = slice(quad * MXU_DIM, (quad + 1) * MXU_DIM)
        qq = q_ref[pl.ds(q0, GRID_W), cols]
        qs = jnp.concatenate([jnp.where(lane_head == h, qq, jnp.zeros_like(qq))
                              for h in range(QUAD)], axis=0)
        return (_dot_nt(k_ref[pl.ds(k0, band), cols], qs)
                + tab_ref[quad, pl.ds(t0, band), :])

    def probs(st):
        return jnp.exp2(st - jnp.max(st, axis=0, keepdims=True)).astype(BF16)

    def values(geom, quad, e):
        k0, q0, _ = geom
        for t in range(QUAD // 2):
            pair = slice(quad * MXU_DIM + t * LANES, quad * MXU_DIM + (t + 1) * LANES)
            rhs = jnp.concatenate([v_ref[pl.ds(k0, band), pair], ones], axis=1)
            o = lax.dot_general(e[:, t * LANES:(t + 1) * LANES], rhs, (((0,), (0,)), ((), ())),
                                preferred_element_type=F32)
            o = o[:, :LANES] / o[:, LANES:]
            obuf[pl.ds(q0, GRID_W), pair] = jnp.where(low, o[:GRID_W], o[GRID_W:]).astype(BF16)

    def one_row(rr, carry):
        geom = geometry(rr)
        following = geometry(jnp.minimum(rr + 1, rows_per_tile - 1))
        st = sbuf[...]
        for quad in range(NA_QUADS):
            if quad + 1 < NA_QUADS:
                st_next = scores(geom, quad + 1)
            else:
                sbuf[...] = scores(following, 0)
            values(geom, quad, probs(st))
            st = st_next
        return carry

    sbuf[...] = scores(geometry(0), 0)
    lax.fori_loop(0, rows_per_tile, one_row, 0, unroll=8)
    o_ref[...] = x_ref[...] + _dot(obuf[...], wo_ref[...])


def _mix1_attn(x, q, k, v, tab, w_out):
    bsz, seq, _ = x.shape
    tq = NA_TQ
    grid_rows = seq // GRID_W
    assert grid_rows >= NA_KH and seq % tq == 0
    tile = pl.BlockSpec((None, tq, D_MODEL), lambda b, i: (b, i, 0))
    whole = pl.BlockSpec((None, seq, NA_WIDTH), lambda b, i: (b, 0, 0),
                         pipeline_mode=pl.Buffered(1))
    return pl.pallas_call(
        functools.partial(_mix1_attn_kernel, grid_rows=grid_rows),
        grid=(bsz, seq // tq),
        in_specs=[tile, tile, whole, whole,
                  _const_spec((NA_QUADS, RPB_H * GRID_W, MXU_DIM)),
                  _const_spec((D_MODEL, D_MODEL))],
        out_specs=tile,
        out_shape=jax.ShapeDtypeStruct(x.shape, F32),
        scratch_shapes=[pltpu.VMEM((tq, NA_WIDTH), BF16),
                        pltpu.VMEM((NA_KH * GRID_W, MXU_DIM), F32)],
        compiler_params=_params("parallel", "arbitrary"),
        name="mix1_attn",
    )(x, q, k, v, tab, w_out)


def _prepare(l0_ffn1_norm, l0_ffn1_w_gate, l0_ffn1_w_up, l0_ffn1_w_down, l0_mix_norm, l0_w_in,
             l0_sgu_ln_g, l0_sgu_ln_b, l0_sgu_w, l0_sgu_b, l0_sink, l0_w_out, l0_ffn2_norm,
             l0_ffn2_w_gate, l0_ffn2_w_up, l0_ffn2_w_down, l1_ffn1_norm, l1_ffn1_w_gate,
             l1_ffn1_w_up, l1_ffn1_w_down, l1_mix_norm, l1_w_qkv, l1_rpb, l1_w_out, l1_ffn2_norm,
             l1_ffn2_w_gate, l1_ffn2_w_up, l1_ffn2_w_down, final_norm):
    def row(v):
        return v.reshape(1, -1)

    def ffn(n, g, u, d):
        return (row(n), g.astype(BF16), u.astype(BF16), d.astype(BF16))

    head_order = [ATT_REP * g + r for r in range(ATT_REP) for g in range(ATT_KV_HEADS)]
    q0 = 2 * SGU_WIDTH
    q_cols = jnp.concatenate([jnp.arange(HEAD_DIM) + q0 + h * HEAD_DIM for h in head_order])
    in_cols = jnp.concatenate([jnp.arange(q0), q_cols, jnp.arange(q0 + Q_WIDTH, AB_IN)])
    out_rows = jnp.concatenate([jnp.arange(SGU_WIDTH), q_cols - q0 + SGU_WIDTH])
    sink_col = jnp.repeat(l0_sink[jnp.array(head_order)], BLOCK).reshape(-1, 1)
    sgu_b = jnp.broadcast_to(l0_sgu_b[:, :, None], (SGU_GROUPS, BLOCK, LANES))
    return dict(
        l0_f1=ffn(l0_ffn1_norm, l0_ffn1_w_gate, l0_ffn1_w_up, l0_ffn1_w_down),
        l0_f2=ffn(l0_ffn2_norm, l0_ffn2_w_gate, l0_ffn2_w_up, l0_ffn2_w_down),
        l1_f1=ffn(l1_ffn1_norm, l1_ffn1_w_gate, l1_ffn1_w_up, l1_ffn1_w_down),
        l1_f2=ffn(l1_ffn2_norm, l1_ffn2_w_gate, l1_ffn2_w_up, l1_ffn2_w_down),
        l0_mix=(row(l0_mix_norm), l0_w_in[:, in_cols].astype(BF16), row(l0_sgu_ln_g),
                row(l0_sgu_ln_b), l0_sgu_w.astype(BF16), sgu_b),
        l0_sink=sink_col,
        l0_w_out=l0_w_out[out_rows].astype(BF16),
        l1_mix=(row(l1_mix_norm), l1_w_qkv.astype(BF16)),
        l1_rpb=l1_rpb,
        l1_w_out=l1_w_out.astype(BF16),
        final_norm=row(final_norm),
    )


def _trunk(x, w, cos, sin, tab):
    bsz, seq, _ = x.shape
    n = bsz * seq
    assert seq % ATT_TQ == 0 and seq % FFN_TM == 0
    fn = w["final_norm"]
    x = _ffn(x.reshape(n, D_MODEL), *w["l0_f1"], fn, final=False)
    a, q, k, v = _mix0_proj(x, *w["l0_mix"], cos, sin, seq)

    def seqs(t):
        return t.reshape(bsz, seq, t.shape[-1])

    x = _mix0_attn(seqs(x), seqs(a), seqs(q), seqs(k), seqs(v), w["l0_sink"], w["l0_w_out"])
    x = _ffn(x.reshape(n, D_MODEL), *w["l0_f2"], fn, final=False)
    x = _ffn(x, *w["l1_f1"], fn, final=False)
    q, k, v = _mix1_proj(x, *w["l1_mix"])
    x = _mix1_attn(seqs(x), seqs(q), seqs(k), seqs(v), tab, w["l1_w_out"])
    x = _ffn(x.reshape(n, D_MODEL), *w["l1_f2"], fn, final=True)
    return x.reshape(bsz, seq, D_MODEL)


def kernel(x_prompt, x_sample, l0_ffn1_norm, l0_ffn1_w_gate, l0_ffn1_w_up, l0_ffn1_w_down, l0_mix_norm, l0_w_in, l0_sgu_ln_g, l0_sgu_ln_b, l0_sgu_w, l0_sgu_b, l0_sink, l0_w_out, l0_ffn2_norm, l0_ffn2_w_gate, l0_ffn2_w_up, l0_ffn2_w_down, l1_ffn1_norm, l1_ffn1_w_gate, l1_ffn1_w_up, l1_ffn1_w_down, l1_mix_norm, l1_w_qkv, l1_rpb, l1_w_out, l1_ffn2_norm, l1_ffn2_w_gate, l1_ffn2_w_up, l1_ffn2_w_down, final_norm):
    w = _prepare(l0_ffn1_norm, l0_ffn1_w_gate, l0_ffn1_w_up, l0_ffn1_w_down, l0_mix_norm, l0_w_in,
                 l0_sgu_ln_g, l0_sgu_ln_b, l0_sgu_w, l0_sgu_b, l0_sink, l0_w_out, l0_ffn2_norm,
                 l0_ffn2_w_gate, l0_ffn2_w_up, l0_ffn2_w_down, l1_ffn1_norm, l1_ffn1_w_gate,
                 l1_ffn1_w_up, l1_ffn1_w_down, l1_mix_norm, l1_w_qkv, l1_rpb, l1_w_out,
                 l1_ffn2_norm, l1_ffn2_w_gate, l1_ffn2_w_up, l1_ffn2_w_down, final_norm)
    cos, sin = _rope_table(max(x_prompt.shape[1], x_sample.shape[1]))
    tab = _na_bias_table(l1_rpb)
    return (_trunk(x_prompt, w, cos, sin, tab), _trunk(x_sample, w, cos, sin, tab))
```

```python
import functools

import jax
import jax.numpy as jnp
from jax import lax
from jax.experimental import pallas as pl
from jax.experimental.pallas import tpu as pltpu

D_MODEL = 1024
D_FF = 2816
BLOCK = 128
SGU_WIDTH = 512
SGU_GROUPS = 4
HEAD_DIM = 64
ATT_HEADS = 8
ATT_KV_HEADS = 2
ATT_REP = ATT_HEADS // ATT_KV_HEADS
WINDOW = 128
ROPE_THETA = 500000.0
ROPE_DIM = HEAD_DIM // 4
ROPE_HALF = ROPE_DIM // 2
Q_WIDTH = ATT_HEADS * HEAD_DIM
KV_WIDTH = ATT_KV_HEADS * HEAD_DIM
AB_IN = 2 * SGU_WIDTH + Q_WIDTH + 2 * KV_WIDTH
NA_HEADS = 16
NA_WIDTH = NA_HEADS * HEAD_DIM
NA_KH = 8
NA_KW = 16
GRID_W = 64
EPS = 1e-6
NEG_INF = -1e30
SCALE = HEAD_DIM ** -0.5
SQRT_HALF = 0.5 ** 0.5
LOG2E = 1.4426950408889634

LANES = 128
MXU_DIM = 256
QUAD = MXU_DIM // HEAD_DIM
NA_QUADS = NA_HEADS // QUAD
RPB_H = 2 * NA_KH - 1
RPB_W = 2 * NA_KW - 1
RPB_CODE = 32

FFN_TM = 1024
FFN_FC = 256
ATT_TQ = 1024
NA_TQ = 1024
VMEM_LIMIT = 56 * 1024 * 1024

F32 = jnp.float32
BF16 = jnp.bfloat16


def _params(*sem):
    return pltpu.CompilerParams(dimension_semantics=sem, vmem_limit_bytes=VMEM_LIMIT)


def _const_spec(shape):
    nd = len(shape)
    return pl.BlockSpec(shape, lambda *_: (0,) * nd, pipeline_mode=pl.Buffered(1))


def _rms(x, g):
    return x * lax.rsqrt(jnp.mean(x * x, axis=-1, keepdims=True) + EPS) * g


def _dot(a, b):
    return jnp.dot(a, b, preferred_element_type=F32)


def _dot_nt(a, b):
    return lax.dot_general(a, b, (((1,), (1,)), ((), ())), preferred_element_type=F32)


def _low_head_lanes(shape):
    return lax.broadcasted_iota(jnp.int32, shape, len(shape) - 1) % LANES < HEAD_DIM


def _ffn_kernel(x_ref, n_ref, wg_ref, wu_ref, wd_ref, fn_ref, o_ref, a_ref, *, final):
    x = x_ref[...]
    h = _rms(x, n_ref[...]).astype(BF16)
    for j in range(D_FF // FFN_FC):
        sl = slice(j * FFN_FC, (j + 1) * FFN_FC)
        g = _dot(h, wg_ref[:, sl])
        u = _dot(h, wu_ref[:, sl])
        a_ref[:, sl] = (g * jax.nn.sigmoid(g) * u).astype(BF16)
    y = x + 0.5 * _dot(a_ref[...], wd_ref[...])
    if final:
        y = _rms(y, fn_ref[...])
    o_ref[...] = y


def _ffn(x, norm, wg, wu, wd, final_norm, *, final):
    n = x.shape[0]
    tile = pl.BlockSpec((FFN_TM, D_MODEL), lambda i: (i, 0))
    return pl.pallas_call(
        functools.partial(_ffn_kernel, final=final),
        grid=(n // FFN_TM,),
        in_specs=[tile, _const_spec((1, D_MODEL)), _const_spec((D_MODEL, D_FF)),
                  _const_spec((D_MODEL, D_FF)), _const_spec((D_FF, D_MODEL)),
                  _const_spec((1, D_MODEL))],
        out_specs=tile,
        out_shape=jax.ShapeDtypeStruct((n, D_MODEL), F32),
        scratch_shapes=[pltpu.VMEM((FFN_TM, D_FF), BF16)],
        compiler_params=_params("parallel"),
        name="ffn_final" if final else "ffn",
    )(x, norm, wg, wu, wd, final_norm)


def _rope_table_kernel(cos_ref, sin_ref):
    rows = cos_ref.shape[0]
    shape = (rows, LANES)
    lane = lax.broadcasted_iota(jnp.int32, shape, 1) % HEAD_DIM
    pos = (lax.broadcasted_iota(jnp.int32, shape, 0) + pl.program_id(0) * rows).astype(F32)
    expo = -(lane % ROPE_HALF).astype(F32) / ROPE_HALF
    inv = jnp.where(lane < ROPE_DIM, jnp.power(ROPE_THETA, expo), 0.0)
    ang = pos * inv
    s = jnp.sin(ang)
    cos_ref[...] = jnp.cos(ang)
    sin_ref[...] = jnp.where(lane < ROPE_HALF, -s, s)


def _rope_table(length):
    rows = 512
    spec = pl.BlockSpec((rows, LANES), lambda i: (i, 0))
    out = jax.ShapeDtypeStruct((length, LANES), F32)
    return pl.pallas_call(
        _rope_table_kernel, grid=(length // rows,), out_specs=[spec, spec], out_shape=[out, out],
        compiler_params=_params("parallel"), name="rope_table")()


def _rope(t, cos, sin):
    lane = lax.broadcasted_iota(jnp.int32, t.shape, 1) % HEAD_DIM
    partner = jnp.where(lane < ROPE_HALF,
                        pltpu.roll(t, LANES - ROPE_HALF, 1),
                        pltpu.roll(t, ROPE_HALF, 1))
    return t * cos + partner * sin


def _mix0_proj_kernel(x_ref, n_ref, win_ref, lng_ref, lnb_ref, sw_ref, sb_ref, cos_ref, sin_ref,
                      a_ref, q_ref, k_ref, v_ref):
    tm = x_ref.shape[0]
    h = _rms(x_ref[...], n_ref[...]).astype(BF16)
    z = _dot(h, win_ref[...])
    zg = z[:, :2 * SGU_WIDTH]
    zs = 0.5 * zg * (1.0 + lax.erf(zg * SQRT_HALF))
    u = zs[:, :SGU_WIDTH]
    vh = zs[:, SGU_WIDTH:]
    mu = jnp.mean(vh, axis=-1, keepdims=True)
    d = vh - mu
    var = jnp.mean(d * d, axis=-1, keepdims=True)
    vn = (d * lax.rsqrt(var + EPS) * lng_ref[...] + lnb_ref[...]).astype(BF16)
    for c in range(tm // BLOCK):
        rows = slice(c * BLOCK, (c + 1) * BLOCK)
        for g in range(SGU_GROUPS):
            cols = slice(g * LANES, (g + 1) * LANES)
            mix = _dot(sw_ref[g], vn[rows, cols]) + sb_ref[g]
            a_ref[rows, cols] = (u[rows, cols] * mix).astype(BF16)
    cos = cos_ref[...]
    sin = sin_ref[...]
    o = 2 * SGU_WIDTH
    for p in range(Q_WIDTH // LANES):
        t = z[:, o + p * LANES:o + (p + 1) * LANES]
        q_ref[:, p * LANES:(p + 1) * LANES] = (_rope(t, cos, sin) * (SCALE * LOG2E)).astype(BF16)
    o += Q_WIDTH
    k_ref[...] = _rope(z[:, o:o + KV_WIDTH], cos, sin).astype(BF16)
    o += KV_WIDTH
    v_ref[...] = z[:, o:o + KV_WIDTH].astype(BF16)


def _mix0_proj(x, norm, w_in, ln_g, ln_b, sgu_w, sgu_b, cos, sin, seq):
    n = x.shape[0]
    tm = FFN_TM
    per_seq = seq // tm

    def tile(w):
        return pl.BlockSpec((tm, w), lambda i: (i, 0))

    def out(w):
        return jax.ShapeDtypeStruct((n, w), BF16)

    pos = pl.BlockSpec((tm, LANES), lambda i: (i % per_seq, 0))
    return pl.pallas_call(
        _mix0_proj_kernel,
        grid=(n // tm,),
        in_specs=[tile(D_MODEL), _const_spec((1, D_MODEL)), _const_spec((D_MODEL, AB_IN)),
                  _const_spec((1, SGU_WIDTH)), _const_spec((1, SGU_WIDTH)),
                  _const_spec((SGU_GROUPS, BLOCK, BLOCK)), _const_spec((SGU_GROUPS, BLOCK, LANES)),
                  pos, pos],
        out_specs=[tile(SGU_WIDTH), tile(Q_WIDTH), tile(KV_WIDTH), tile(KV_WIDTH)],
        out_shape=[out(SGU_WIDTH), out(Q_WIDTH), out(KV_WIDTH), out(KV_WIDTH)],
        compiler_params=_params("parallel"),
        name="mix0_proj",
    )(x, norm, w_in, ln_g, ln_b, sgu_w, sgu_b, cos, sin)


def _mix0_attn_kernel(x_ref, a_ref, q_ref, kp_ref, kc_ref, kn_ref, vp_ref, vc_ref, vn_ref,
                      sink_ref, wo_ref, o_ref, kbuf, vbuf, bbuf, *, seq):
    tq = x_ref.shape[0]
    pos0 = pl.program_id(1) * tq
    kbuf[0:BLOCK] = kp_ref[...]
    kbuf[BLOCK:BLOCK + tq] = kc_ref[...]
    kbuf[BLOCK + tq:] = kn_ref[...]
    vbuf[0:BLOCK] = vp_ref[...]
    vbuf[BLOCK:BLOCK + tq] = vc_ref[...]
    vbuf[BLOCK + tq:] = vn_ref[...]
    nslab = ATT_REP * ATT_KV_HEADS
    kw_len = 3 * BLOCK
    low = _low_head_lanes((BLOCK, LANES))
    qi = lax.broadcasted_iota(jnp.int32, (BLOCK, kw_len), 0)
    kj = lax.broadcasted_iota(jnp.int32, (BLOCK, kw_len), 1)
    in_band = jnp.abs(kj - BLOCK - qi) <= WINDOW
    sink = sink_ref[...] * LOG2E
    ones = jnp.ones((kw_len, LANES), BF16)
    half = nslab * BLOCK // 2

    def scores(b):
        rows = slice(b * BLOCK, (b + 1) * BLOCK)
        kpos = pos0 + (b - 1) * BLOCK + kj
        valid = in_band & (kpos >= 0) & (kpos < seq)
        slabs = []
        for r in range(ATT_REP):
            qp = q_ref[rows, r * LANES:(r + 1) * LANES]
            slabs.append(jnp.where(low, qp, jnp.zeros_like(qp)))
            slabs.append(jnp.where(low, jnp.zeros_like(qp), qp))
        s = _dot_nt(jnp.concatenate(slabs, axis=0), kbuf[b * BLOCK:b * BLOCK + kw_len])
        s = jnp.where(valid[None], s.reshape(nslab, BLOCK, kw_len), NEG_INF)
        return s.reshape(nslab * BLOCK, kw_len)

    def finish(b, s):
        rows = slice(b * BLOCK, (b + 1) * BLOCK)
        m = jnp.maximum(jnp.max(s, axis=-1, keepdims=True), sink)
        e = jnp.exp2(s - m).astype(BF16)
        rhs = jnp.concatenate([vbuf[b * BLOCK:b * BLOCK + kw_len], ones], axis=1)
        o = jnp.concatenate([_dot(e[i * half:(i + 1) * half], rhs) for i in range(2)], axis=0)
        o = o[:, :LANES] / (o[:, LANES:] + jnp.exp2(sink - m))
        for r in range(ATT_REP):
            o_lo = o[(2 * r) * BLOCK:(2 * r + 1) * BLOCK]
            o_hi = o[(2 * r + 1) * BLOCK:(2 * r + 2) * BLOCK]
            bbuf[rows, r * LANES:(r + 1) * LANES] = jnp.where(low, o_lo, o_hi).astype(BF16)

    nblk = tq // BLOCK
    s = scores(0)
    for b in range(nblk):
        s_next = scores(b + 1) if b + 1 < nblk else None
        finish(b, s)
        s = s_next
    y = _dot(a_ref[...], wo_ref[:SGU_WIDTH]) + _dot(bbuf[...], wo_ref[SGU_WIDTH:])
    o_ref[...] = x_ref[...] + y


def _mix0_attn(x, a, q, k, v, sink_col, w_out):
    bsz, seq, _ = x.shape
    tq = ATT_TQ
    per = tq // BLOCK
    nblk = seq // BLOCK

    def tile(w):
        return pl.BlockSpec((None, tq, w), lambda b, i: (b, i, 0))

    prev = pl.BlockSpec((None, BLOCK, KV_WIDTH), lambda b, i: (b, jnp.maximum(i * per - 1, 0), 0))
    nxt = pl.BlockSpec((None, BLOCK, KV_WIDTH),
                       lambda b, i: (b, jnp.minimum((i + 1) * per, nblk - 1), 0))
    return pl.pallas_call(
        functools.partial(_mix0_attn_kernel, seq=seq),
        grid=(bsz, seq // tq),
        in_specs=[tile(D_MODEL), tile(SGU_WIDTH), tile(Q_WIDTH),
                  prev, tile(KV_WIDTH), nxt, prev, tile(KV_WIDTH), nxt,
                  _const_spec((ATT_HEADS * BLOCK, 1)), _const_spec((D_MODEL, D_MODEL))],
        out_specs=tile(D_MODEL),
        out_shape=jax.ShapeDtypeStruct(x.shape, F32),
        scratch_shapes=[pltpu.VMEM((tq + 2 * BLOCK, KV_WIDTH), BF16),
                        pltpu.VMEM((tq + 2 * BLOCK, KV_WIDTH), BF16),
                        pltpu.VMEM((tq, Q_WIDTH), BF16)],
        compiler_params=_params("parallel", "parallel"),
        name="mix0_attn",
    )(x, a, q, k, k, k, v, v, v, sink_col, w_out)


def _mix1_proj_kernel(x_ref, n_ref, w_ref, q_ref, k_ref, v_ref):
    h = _rms(x_ref[...], n_ref[...]).astype(BF16)
    q_ref[...] = (_dot(h, w_ref[:, :NA_WIDTH]) * (SCALE * LOG2E)).astype(BF16)
    k_ref[...] = _dot(h, w_ref[:, NA_WIDTH:2 * NA_WIDTH]).astype(BF16)
    v_ref[...] = _dot(h, w_ref[:, 2 * NA_WIDTH:]).astype(BF16)


def _mix1_proj(x, norm, w_qkv):
    n = x.shape[0]
    tile = pl.BlockSpec((FFN_TM, D_MODEL), lambda i: (i, 0))
    out = jax.ShapeDtypeStruct((n, NA_WIDTH), BF16)
    return pl.pallas_call(
        _mix1_proj_kernel,
        grid=(n // FFN_TM,),
        in_specs=[tile, _const_spec((1, D_MODEL)), _const_spec((D_MODEL, 3 * NA_WIDTH))],
        out_specs=[tile, tile, tile],
        out_shape=[out, out, out],
        compiler_params=_params("parallel"),
        name="mix1_proj",
    )(x, norm, w_qkv)


def _na_bias_kernel(rpb_ref, tab_ref):
    quad = pl.program_id(0)
    shape = (GRID_W, MXU_DIM)
    kc = lax.broadcasted_iota(jnp.int32, shape, 0)
    lane = lax.broadcasted_iota(jnp.int32, shape, 1)
    c = lane % GRID_W
    start = jnp.clip(c - NA_KW // 2, 0, GRID_W - NA_KW)
    in_win = (kc >= start) & (kc < start + NA_KW)
    code = jnp.where(in_win, (lane // GRID_W) * RPB_CODE + (kc - c + NA_KW - 1), -1)
    for dr in range(RPB_H):
        val = jnp.full(shape, NEG_INF, F32)
        for h in range(QUAD):
            base = ((quad * QUAD + h) * RPB_H + dr) * RPB_W
            for dcol in range(RPB_W):
                val = jnp.where(code == h * RPB_CODE + dcol, rpb_ref[base + dcol] * LOG2E, val)
        tab_ref[dr * GRID_W:(dr + 1) * GRID_W, :] = val


def _na_bias_table(rpb):
    return pl.pallas_call(
        _na_bias_kernel,
        grid=(NA_QUADS,),
        in_specs=[pl.BlockSpec(memory_space=pltpu.SMEM)],
        out_specs=pl.BlockSpec((None, RPB_H * GRID_W, MXU_DIM), lambda p: (p, 0, 0)),
        out_shape=jax.ShapeDtypeStruct((NA_QUADS, RPB_H * GRID_W, MXU_DIM), F32),
        compiler_params=_params("parallel"),
        name="na_bias_table",
    )(rpb.reshape(-1))


def _mix1_attn_kernel(x_ref, q_ref, k_ref, v_ref, tab_ref, wo_ref, o_ref, obuf, sbuf, *, grid_rows):
    tq = x_ref.shape[0]
    rows_per_tile = tq // GRID_W
    band = NA_KH * GRID_W
    row0 = pl.program_id(1) * rows_per_tile
    lane_head = lax.broadcasted_iota(jnp.int32, (GRID_W, MXU_DIM), 1) // HEAD_DIM
    low = _low_head_lanes((GRID_W, LANES))
    ones = jnp.ones((band, LANES), BF16)

    def geometry(rr):
        r = row0 + rr
        rs = jnp.clip(r - NA_KH // 2, 0, grid_rows - NA_KH)
        return (pl.multiple_of(rs * GRID_W, GRID_W), pl.multiple_of(rr * GRID_W, GRID_W),
                pl.multiple_of((rs - r + NA_KH - 1) * GRID_W, GRID_W))

    def scores(geom, quad):
        k0, q0, t0 = geom
        cols = slice(quad * MXU_DIM, (quad + 1) * MXU_DIM)
        qq = q_ref[pl.ds(q0, GRID_W), cols]
        qs = jnp.concatenate([jnp.where(lane_head == h, qq, jnp.zeros_like(qq))
                              for h in range(QUAD)], axis=0)
        return (_dot_nt(k_ref[pl.ds(k0, band), cols], qs)
                + tab_ref[quad, pl.ds(t0, band), :])

    def probs(st):
        return jnp.exp2(st - jnp.max(st, axis=0, keepdims=True)).astype(BF16)

    def values(geom, quad, e):
        k0, q0, _ = geom
        for t in range(QUAD // 2):
            pair = slice(quad * MXU_DIM + t * LANES, quad * MXU_DIM + (t + 1) * LANES)
            rhs = jnp.concatenate([v_ref[pl.ds(k0, band), pair], ones], axis=1)
            o = lax.dot_general(e[:, t * LANES:(t + 1) * LANES], rhs, (((0,), (0,)), ((), ())),
                                preferred_element_type=F32)
            o = o[:, :LANES] / o[:, LANES:]
            obuf[pl.ds(q0, GRID_W), pair] = jnp.where(low, o[:GRID_W], o[GRID_W:]).astype(BF16)

    def one_row(rr, carry):
        geom = geometry(rr)
        following = geometry(jnp.minimum(rr + 1, rows_per_tile - 1))
        st = sbuf[...]
        for quad in range(NA_QUADS):
            if quad + 1 < NA_QUADS:
                st_next = scores(geom, quad + 1)
            else:
                sbuf[...] = scores(following, 0)
            values(geom, quad, probs(st))
            st = st_next
        return carry

    sbuf[...] = scores(geometry(0), 0)
    lax.fori_loop(0, rows_per_tile, one_row, 0, unroll=8)
    o_ref[...] = x_ref[...] + _dot(obuf[...], wo_ref[...])


def _mix1_attn(x, q, k, v, tab, w_out):
    bsz, seq, _ = x.shape
    tq = NA_TQ
    grid_rows = seq // GRID_W
    assert grid_rows >= NA_KH and seq % tq == 0
    tile = pl.BlockSpec((None, tq, D_MODEL), lambda b, i: (b, i, 0))
    whole = pl.BlockSpec((None, seq, NA_WIDTH), lambda b, i: (b, 0, 0),
                         pipeline_mode=pl.Buffered(1))
    return pl.pallas_call(
        functools.partial(_mix1_attn_kernel, grid_rows=grid_rows),
        grid=(bsz, seq // tq),
        in_specs=[tile, tile, whole, whole,
                  _const_spec((NA_QUADS, RPB_H * GRID_W, MXU_DIM)),
                  _const_spec((D_MODEL, D_MODEL))],
        out_specs=tile,
        out_shape=jax.ShapeDtypeStruct(x.shape, F32),
        scratch_shapes=[pltpu.VMEM((tq, NA_WIDTH), BF16),
                        pltpu.VMEM((NA_KH * GRID_W, MXU_DIM), F32)],
        compiler_params=_params("parallel", "arbitrary"),
        name="mix1_attn",
    )(x, q, k, v, tab, w_out)


def _prepare(l0_ffn1_norm, l0_ffn1_w_gate, l0_ffn1_w_up, l0_ffn1_w_down, l0_mix_norm, l0_w_in,
             l0_sgu_ln_g, l0_sgu_ln_b, l0_sgu_w, l0_sgu_b, l0_sink, l0_w_out, l0_ffn2_norm,
             l0_ffn2_w_gate, l0_ffn2_w_up, l0_ffn2_w_down, l1_ffn1_norm, l1_ffn1_w_gate,
             l1_ffn1_w_up, l1_ffn1_w_down, l1_mix_norm, l1_w_qkv, l1_rpb, l1_w_out, l1_ffn2_norm,
             l1_ffn2_w_gate, l1_ffn2_w_up, l1_ffn2_w_down, final_norm):
    def row(v):
        return v.reshape(1, -1)

    def ffn(n, g, u, d):
        return (row(n), g.astype(BF16), u.astype(BF16), d.astype(BF16))

    head_order = [ATT_REP * g + r for r in range(ATT_REP) for g in range(ATT_KV_HEADS)]
    q0 = 2 * SGU_WIDTH
    q_cols = jnp.concatenate([jnp.arange(HEAD_DIM) + q0 + h * HEAD_DIM for h in head_order])
    in_cols = jnp.concatenate([jnp.arange(q0), q_cols, jnp.arange(q0 + Q_WIDTH, AB_IN)])
    out_rows = jnp.concatenate([jnp.arange(SGU_WIDTH), q_cols - q0 + SGU_WIDTH])
    sink_col = jnp.repeat(l0_sink[jnp.array(head_order)], BLOCK).reshape(-1, 1)
    sgu_b = jnp.broadcast_to(l0_sgu_b[:, :, None], (SGU_GROUPS, BLOCK, LANES))
    return dict(
        l0_f1=ffn(l0_ffn1_norm, l0_ffn1_w_gate, l0_ffn1_w_up, l0_ffn1_w_down),
        l0_f2=ffn(l0_ffn2_norm, l0_ffn2_w_gate, l0_ffn2_w_up, l0_ffn2_w_down),
        l1_f1=ffn(l1_ffn1_norm, l1_ffn1_w_gate, l1_ffn1_w_up, l1_ffn1_w_down),
        l1_f2=ffn(l1_ffn2_norm, l1_ffn2_w_gate, l1_ffn2_w_up, l1_ffn2_w_down),
        l0_mix=(row(l0_mix_norm), l0_w_in[:, in_cols].astype(BF16), row(l0_sgu_ln_g),
                row(l0_sgu_ln_b), l0_sgu_w.astype(BF16), sgu_b),
        l0_sink=sink_col,
        l0_w_out=l0_w_out[out_rows].astype(BF16),
        l1_mix=(row(l1_mix_norm), l1_w_qkv.astype(BF16)),
        l1_rpb=l1_rpb,
        l1_w_out=l1_w_out.astype(BF16),
        final_norm=row(final_norm),
    )


def _trunk(x, w, cos, sin, tab):
    bsz, seq, _ = x.shape
    n = bsz * seq
    assert seq % ATT_TQ == 0 and seq % FFN_TM == 0
    fn = w["final_norm"]
    x = _ffn(x.reshape(n, D_MODEL), *w["l0_f1"], fn, final=False)
    a, q, k, v = _mix0_proj(x, *w["l0_mix"], cos, sin, seq)

    def seqs(t):
        return t.reshape(bsz, seq, t.shape[-1])

    x = _mix0_attn(seqs(x), seqs(a), seqs(q), seqs(k), seqs(v), w["l0_sink"], w["l0_w_out"])
    x = _ffn(x.reshape(n, D_MODEL), *w["l0_f2"], fn, final=False)
    x = _ffn(x, *w["l1_f1"], fn, final=False)
    q, k, v = _mix1_proj(x, *w["l1_mix"])
    x = _mix1_attn(seqs(x), seqs(q), seqs(k), seqs(v), tab, w["l1_w_out"])
    x = _ffn(x.reshape(n, D_MODEL), *w["l1_f2"], fn, final=True)
    return x.reshape(bsz, seq, D_MODEL)


def kernel(x_prompt, x_sample, l0_ffn1_norm, l0_ffn1_w_gate, l0_ffn1_w_up, l0_ffn1_w_down, l0_mix_norm, l0_w_in, l0_sgu_ln_g, l0_sgu_ln_b, l0_sgu_w, l0_sgu_b, l0_sink, l0_w_out, l0_ffn2_norm, l0_ffn2_w_gate, l0_ffn2_w_up, l0_ffn2_w_down, l1_ffn1_norm, l1_ffn1_w_gate, l1_ffn1_w_up, l1_ffn1_w_down, l1_mix_norm, l1_w_qkv, l1_rpb, l1_w_out, l1_ffn2_norm, l1_ffn2_w_gate, l1_ffn2_w_up, l1_ffn2_w_down, final_norm):
    w = _prepare(l0_ffn1_norm, l0_ffn1_w_gate, l0_ffn1_w_up, l0_ffn1_w_down, l0_mix_norm, l0_w_in,
                 l0_sgu_ln_g, l0_sgu_ln_b, l0_sgu_w, l0_sgu_b, l0_sink, l0_w_out, l0_ffn2_norm,
                 l0_ffn2_w_gate, l0_ffn2_w_up, l0_ffn2_w_down, l1_ffn1_norm, l1_ffn1_w_gate,
                 l1_ffn1_w_up, l1_ffn1_w_down, l1_mix_norm, l1_w_qkv, l1_rpb, l1_w_out,
                 l1_ffn2_norm, l1_ffn2_w_gate, l1_ffn2_w_up, l1_ffn2_w_down, final_norm)
    cos, sin = _rope_table(max(x_prompt.shape[1], x_sample.shape[1]))
    tab = _na_bias_table(l1_rpb)
    return (_trunk(x_prompt, w, cos, sin, tab), _trunk(x_sample, w, cos, sin, tab))
```
